```python
import math
import jax, jax.numpy as jnp
from jax import lax
import numpy as np

D_MODEL = 2048
BATCH = 1
SEQ = 8192
DEPTH = 2

CHUNK = 64
Q_BLOCK = 128
EPS = 1e-6

MLA_HEADS = 16
MLA_Q_RANK = 512
MLA_KV_RANK = 512
MLA_NOPE = 128
MLA_ROPE = 64
MLA_V = 128
ROPE_THETA = 10000.0

HG_HEADS = 16
HG_DK = 128
HG_DV = 128
HG_WIDTH = HG_HEADS * HG_DK

SSM_EXPAND = 2
SSM_INNER = SSM_EXPAND * D_MODEL
SSM_HEADDIM = 64
SSM_HEADS = SSM_INNER // SSM_HEADDIM
SSM_GROUPS = 8
SSM_STATE = 128
SSM_CONV = 4
SSM_CONV_DIM = SSM_INNER + 2 * SSM_GROUPS * SSM_STATE

D_FF = 5632
N_BRANCH = 3

IN_SIZES = (MLA_Q_RANK, MLA_KV_RANK, MLA_ROPE,
            HG_WIDTH, HG_WIDTH, HG_WIDTH, HG_WIDTH,
            SSM_INNER, SSM_CONV_DIM, SSM_HEADS,
            N_BRANCH * D_MODEL)
IN_DIM = int(sum(IN_SIZES))
IN_SPLITS = tuple(int(v) for v in np.cumsum(IN_SIZES)[:-1])

kernel_name = "hybrid_mla_hgrn2_mamba2_macaron"


def rmsnorm(x, w):
    xf = x.astype(jnp.float32)
    y = xf * lax.rsqrt(jnp.mean(xf * xf, axis=-1, keepdims=True) + EPS)
    return (y * w.astype(jnp.float32)).astype(x.dtype)


def swiglu(x, w_gate_up, w_down):
    g, u = jnp.split(x @ w_gate_up, 2, axis=-1)
    return (jax.nn.silu(g) * u) @ w_down


def rope_tables(seq, dim):
    inv = 1.0 / (ROPE_THETA ** (jnp.arange(0, dim, 2, dtype=jnp.float32) / dim))
    ang = jnp.arange(seq, dtype=jnp.float32)[:, None] * inv[None, :]
    return jnp.cos(ang), jnp.sin(ang)


def apply_rope(x, cos, sin):
    x1, x2 = jnp.split(x.astype(jnp.float32), 2, axis=-1)
    return jnp.concatenate([x1 * cos - x2 * sin, x1 * sin + x2 * cos], axis=-1).astype(x.dtype)


def tril_mask():
    return jnp.tril(jnp.ones((CHUNK, CHUNK), dtype=bool))


def mla_branch(q_lat, kv_lat, k_pe, q_norm_w, w_uq, kv_norm_w, w_ukv, cos, sin):
    B, S, _ = q_lat.shape
    q = (rmsnorm(q_lat, q_norm_w) @ w_uq).reshape(B, S, MLA_HEADS, MLA_NOPE + MLA_ROPE)
    q_nope = q[..., :MLA_NOPE]
    q_pe = apply_rope(q[..., MLA_NOPE:], cos[:, None, :], sin[:, None, :])
    kv = (rmsnorm(kv_lat, kv_norm_w) @ w_ukv).reshape(B, S, MLA_HEADS, MLA_NOPE + MLA_V)
    k_nope, v = kv[..., :MLA_NOPE], kv[..., MLA_NOPE:]
    k_rot = apply_rope(k_pe, cos, sin)
    scale = (MLA_NOPE + MLA_ROPE) ** -0.5
    n_blk = S // Q_BLOCK
    qn_b = q_nope.reshape(B, n_blk, Q_BLOCK, MLA_HEADS, MLA_NOPE).transpose(1, 0, 2, 3, 4)
    qp_b = q_pe.reshape(B, n_blk, Q_BLOCK, MLA_HEADS, MLA_ROPE).transpose(1, 0, 2, 3, 4)
    key_chunk = jnp.arange(S) // CHUNK

    def block(args):
        i, qn, qp = args
        s = (jnp.einsum('bqhd,bkhd->bhqk', qn, k_nope, preferred_element_type=jnp.float32)
             + jnp.einsum('bqhd,bkd->bhqk', qp, k_rot, preferred_element_type=jnp.float32)) * scale
        q_chunk = (i * Q_BLOCK + jnp.arange(Q_BLOCK)) // CHUNK
        mask = key_chunk[None, :] <= q_chunk[:, None]
        p = jax.nn.softmax(jnp.where(mask, s, -jnp.inf), axis=-1).astype(v.dtype)
        return jnp.einsum('bhqk,bkhd->bqhd', p, v)

    o = lax.map(block, (jnp.arange(n_blk), qn_b, qp_b))
    return o.transpose(1, 0, 2, 3, 4).reshape(B, S, MLA_HEADS * MLA_V)


def hgrn2_branch(q_in, f_in, i_in, g_in, lb, norm_w):
    B, S, _ = q_in.shape
    nc = S // CHUNK
    f32 = jnp.float32
    tril = tril_mask()

    def chunks(t, d):
        return t.astype(f32).reshape(B, nc, CHUNK, HG_HEADS, d).transpose(1, 0, 3, 2, 4)

    f = lb + (1.0 - lb) * jax.nn.sigmoid(f_in.astype(f32))
    q = chunks(jax.nn.silu(q_in.astype(f32)) * HG_DK ** -0.5, HG_DK)
    k = chunks(1.0 - f, HG_DK)
    v = chunks(i_in, HG_DV)
    b = jnp.cumsum(chunks(jnp.log(f), HG_DK), axis=3)

    def step(state, inp):
        qc, kc, vc, bc = inp
        b_last = bc[:, :, -1:, :]
        o_inter = jnp.einsum('bhtk,bhkv->bhtv', qc * jnp.exp(bc), state)
        diff = jnp.where(tril[:, :, None], bc[:, :, :, None, :] - bc[:, :, None, :, :], -jnp.inf)
        att = jnp.einsum('bhtk,bhsk,bhtsk->bhts', qc, kc, jnp.exp(diff))
        o = o_inter + jnp.einsum('bhts,bhsv->bhtv', att, vc)
        state = (jnp.exp(b_last[:, :, 0, :, None]) * state
                 + jnp.einsum('bhsk,bhsv->bhkv', kc * jnp.exp(b_last - bc), vc))
        return state, o

    s0 = jnp.zeros((B, HG_HEADS, HG_DK, HG_DV), f32)
    _, o = lax.scan(step, s0, (q, k, v, b))
    o = o.transpose(1, 0, 3, 2, 4).reshape(B, S, HG_HEADS, HG_DV)
    o = o * lax.rsqrt(jnp.mean(o * o, axis=-1, keepdims=True) + EPS)
    o = o.reshape(B, S, HG_WIDTH) * norm_w.astype(f32) * jax.nn.silu(g_in.astype(f32))
    return o.astype(q_in.dtype)


def mamba2_branch(z, xbc, dt_raw, conv_w, conv_b, a_log, dt_bias, d_skip, norm_w):
    B, S, _ = xbc.shape
    nc = S // CHUNK
    hg = SSM_HEADS // SSM_GROUPS
    f32 = jnp.float32
    tril = tril_mask()
    xbc = lax.conv_general_dilated(xbc, conv_w[:, None, :].astype(xbc.dtype), window_strides=(1,),
                                   padding=[(SSM_CONV - 1, 0)], dimension_numbers=('NWC', 'WIO', 'NWC'),
                                   feature_group_count=SSM_CONV_DIM) + conv_b
    xbc = jax.nn.silu(xbc)
    xs, bm, cm = jnp.split(xbc, [SSM_INNER, SSM_INNER + SSM_GROUPS * SSM_STATE], axis=-1)
    xs = xs.astype(f32).reshape(B, nc, CHUNK, SSM_GROUPS, hg, SSM_HEADDIM)
    bm = bm.astype(f32).reshape(B, nc, CHUNK, SSM_GROUPS, SSM_STATE)
    cm = cm.astype(f32).reshape(B, nc, CHUNK, SSM_GROUPS, SSM_STATE)
    dt = jax.nn.softplus(dt_raw.astype(f32) + dt_bias.astype(f32)).reshape(B, nc, CHUNK, SSM_GROUPS, hg)
    a = -jnp.exp(a_log.astype(f32)).reshape(SSM_GROUPS, hg)
    a_cum = jnp.cumsum((dt * a).transpose(0, 3, 4, 1, 2), axis=-1)
    xdt = xs * dt[..., None]
    seg = a_cum[..., :, None] - a_cum[..., None, :]
    decay = jnp.exp(jnp.where(tril, seg, -jnp.inf))
    cb = jnp.einsum('bclgn,bcsgn->bgcls', cm, bm)
    y_diag = jnp.einsum('bgcls,bghcls,bcsghp->bclghp', cb, decay, xdt)
    decay_states = jnp.exp(a_cum[..., -1:] - a_cum)
    states = jnp.einsum('bcsgn,bghcs,bcsghp->cbghpn', bm, decay_states, xdt)
    chunk_decay = jnp.exp(a_cum[..., -1]).transpose(3, 0, 1, 2)

    def step(h, inp):
        st, dec = inp
        return dec[..., None, None] * h + st, h

    _, h_prev = lax.scan(step, jnp.zeros(states.shape[1:], f32), (states, chunk_decay))
    y_off = jnp.einsum('bclgn,cbghpn,bghcl->bclghp', cm, h_prev, jnp.exp(a_cum))
    y = y_diag + y_off + xs * d_skip.astype(f32).reshape(SSM_GROUPS, hg)[:, :, None]
    y = y.reshape(B, S, SSM_INNER) * jax.nn.silu(z.astype(f32))
    y = y.reshape(B, S, SSM_GROUPS, SSM_INNER // SSM_GROUPS)
    y = y * lax.rsqrt(jnp.mean(y * y, axis=-1, keepdims=True) + EPS)
    y = y.reshape(B, S, SSM_INNER) * norm_w.astype(f32)
    return y.astype(z.dtype)


def setup_inputs(seed: int = 0) -> dict:
    key = jax.random.key(seed)
    ks = iter(jax.random.split(key, 32))
    L = DEPTH

    def nrm(shape, fan_in):
        return jax.random.normal(next(ks), shape, jnp.float32) * fan_in ** -0.5

    def gain(shape):
        return 1.0 + 0.01 * jax.random.normal(next(ks), shape, jnp.float32)

    x = jax.random.normal(next(ks), (BATCH, SEQ, D_MODEL), jnp.float32)
    ffn1_norm = gain((L, D_MODEL))
    ffn1_wi = nrm((L, D_MODEL, 2 * D_FF), D_MODEL)
    ffn1_wo = nrm((L, D_FF, D_MODEL), D_FF)
    mix_norm = gain((L, D_MODEL))
    w_in = nrm((L, D_MODEL, IN_DIM), D_MODEL)
    mla_q_norm = gain((L, MLA_Q_RANK))
    mla_w_uq = nrm((L, MLA_Q_RANK, MLA_HEADS * (MLA_NOPE + MLA_ROPE)), MLA_Q_RANK)
    mla_kv_norm = gain((L, MLA_KV_RANK))
    mla_w_ukv = nrm((L, MLA_KV_RANK, MLA_HEADS * (MLA_NOPE + MLA_V)), MLA_KV_RANK)
    hgrn_lb_logits = 0.5 * jax.random.normal(next(ks), (L, HG_WIDTH), jnp.float32)
    hgrn_norm = gain((L, HG_WIDTH))
    ssm_conv_w = nrm((L, SSM_CONV, SSM_CONV_DIM), SSM_CONV)
    ssm_conv_b = 0.01 * jax.random.normal(next(ks), (L, SSM_CONV_DIM), jnp.float32)
    ssm_a_log = jnp.log(jax.random.uniform(next(ks), (L, SSM_HEADS), jnp.float32, 1.0, 16.0))
    dt0 = jnp.exp(jax.random.uniform(next(ks), (L, SSM_HEADS), jnp.float32, math.log(1e-3), math.log(1e-1)))
    ssm_dt_bias = dt0 + jnp.log(-jnp.expm1(-dt0))
    ssm_d = gain((L, SSM_HEADS))
    ssm_norm = gain((L, SSM_INNER))
    w_o_mla = nrm((L, MLA_HEADS * MLA_V, D_MODEL), MLA_HEADS * MLA_V)
    w_o_hgrn = nrm((L, HG_WIDTH, D_MODEL), HG_WIDTH)
    w_o_ssm = nrm((L, SSM_INNER, D_MODEL), SSM_INNER)
    w_out = nrm((L, D_MODEL, D_MODEL), D_MODEL)
    ffn2_norm = gain((L, D_MODEL))
    ffn2_wi = nrm((L, D_MODEL, 2 * D_FF), D_MODEL)
    ffn2_wo = nrm((L, D_FF, D_MODEL), D_FF)
    final_norm = gain((D_MODEL,))
    return {"x": x, "ffn1_norm": ffn1_norm, "ffn1_wi": ffn1_wi, "ffn1_wo": ffn1_wo,
            "mix_norm": mix_norm, "w_in": w_in, "mla_q_norm": mla_q_norm, "mla_w_uq": mla_w_uq,
            "mla_kv_norm": mla_kv_norm, "mla_w_ukv": mla_w_ukv, "hgrn_lb_logits": hgrn_lb_logits,
            "hgrn_norm": hgrn_norm, "ssm_conv_w": ssm_conv_w, "ssm_conv_b": ssm_conv_b,
            "ssm_a_log": ssm_a_log, "ssm_dt_bias": ssm_dt_bias, "ssm_d": ssm_d, "ssm_norm": ssm_norm,
            "w_o_mla": w_o_mla, "w_o_hgrn": w_o_hgrn, "w_o_ssm": w_o_ssm, "w_out": w_out,
            "ffn2_norm": ffn2_norm, "ffn2_wi": ffn2_wi, "ffn2_wo": ffn2_wo, "final_norm": final_norm}


def reference(x, ffn1_norm, ffn1_wi, ffn1_wo, mix_norm, w_in, mla_q_norm, mla_w_uq, mla_kv_norm,
              mla_w_ukv, hgrn_lb_logits, hgrn_norm, ssm_conv_w, ssm_conv_b, ssm_a_log, ssm_dt_bias,
              ssm_d, ssm_norm, w_o_mla, w_o_hgrn, w_o_ssm, w_out, ffn2_norm, ffn2_wi, ffn2_wo,
              final_norm):
    S = x.shape[1]
    cos, sin = rope_tables(S, MLA_ROPE)
    p = jax.nn.softmax(hgrn_lb_logits.astype(jnp.float32), axis=0)
    lower_bounds = jnp.cumsum(p, axis=0) - p[0:1]
    for l in range(DEPTH):
        x = x + 0.5 * swiglu(rmsnorm(x, ffn1_norm[l]), ffn1_wi[l], ffn1_wo[l])
        h = rmsnorm(x, mix_norm[l])
        (q_lat, kv_lat, k_pe, hq, hf, hi, hgate, z, xbc, dt_raw, gates) = jnp.split(h @ w_in[l], IN_SPLITS, axis=-1)
        y_a = mla_branch(q_lat, kv_lat, k_pe, mla_q_norm[l], mla_w_uq[l], mla_kv_norm[l], mla_w_ukv[l], cos, sin) @ w_o_mla[l]
        y_b = hgrn2_branch(hq, hf, hi, hgate, lower_bounds[l], hgrn_norm[l]) @ w_o_hgrn[l]
        y_c = mamba2_branch(z, xbc, dt_raw, ssm_conv_w[l], ssm_conv_b[l], ssm_a_log[l], ssm_dt_bias[l],
                            ssm_d[l], ssm_norm[l]) @ w_o_ssm[l]
        g_a, g_b, g_c = jnp.split(jax.nn.sigmoid(gates), N_BRANCH, axis=-1)
        x = x + (g_a * y_a + g_b * y_b + g_c * y_c) @ w_out[l]
        x = x + 0.5 * swiglu(rmsnorm(x, ffn2_norm[l]), ffn2_wi[l], ffn2_wo[l])
    return rmsnorm(x, final_norm)
```

```python
import functools

import numpy as np
import jax
import jax.numpy as jnp
from jax import lax
from jax.experimental import pallas as pl
from jax.experimental.pallas import tpu as pltpu

F32 = jnp.float32
BF16 = jnp.bfloat16

D_MODEL = 2048
DEPTH = 2
CHUNK = 64
EPS = 1e-6

MLA_HEADS = 16
MLA_RANK = 512
MLA_NOPE = 128
MLA_ROPE = 64
MLA_V = 128
MLA_QK_PAD = 256
ROPE_THETA = 10000.0

HG_HEADS = 16
HG_D = 128
HG_WIDTH = HG_HEADS * HG_D
HG_LEVELS = 7

SSM_INNER = 4096
SSM_P = 64
SSM_HEADS = 64
SSM_GROUPS = 8
SSM_HPG = SSM_HEADS // SSM_GROUPS
SSM_N = 128
SSM_GW = SSM_INNER // SSM_GROUPS
SSM_CONV = 4
SSM_CONV_DIM = SSM_INNER + 2 * SSM_GROUPS * SSM_N
ODD_SLOT = 8

D_FF = 5632

LANES = 128
SUBLANES = 8
VMEM_LIMIT = 52 * 1024 * 1024


def _cparams(n_axes):
    return pltpu.CompilerParams(
        dimension_semantics=("arbitrary",) * n_axes, vmem_limit_bytes=VMEM_LIMIT)


def _sigmoid(x):
    return jax.nn.sigmoid(x)


def _silu(x):
    return x * jax.nn.sigmoid(x)


def _softplus(x):
    return jnp.maximum(x, 0.0) + jnp.log1p(jnp.exp(-jnp.abs(x)))


def _split_bf16(x):
    hi = x.astype(BF16)
    mid = (x - hi.astype(F32)).astype(BF16)
    return jnp.concatenate([hi, mid], axis=1)


def _dot(a, b):
    return jnp.dot(a, b, preferred_element_type=F32)


def _dot_nt(a, b):
    return lax.dot_general(a, b, (((1,), (1,)), ((), ())), preferred_element_type=F32)


def _dot_tn(a, b):
    return lax.dot_general(a, b, (((0,), (0,)), ((), ())), preferred_element_type=F32)


def _norm_mm_kernel(*refs, n_w, n_ex, epi):
    x_ref, nw_ref = refs[0], refs[1]
    w_refs = refs[2:2 + n_w]
    ex_refs = refs[2 + n_w:2 + n_w + n_ex]
    o_ref = refs[2 + n_w + n_ex]
    xn_ref = refs[3 + n_w + n_ex]

    @pl.when(pl.program_id(1) == 0)
    def _():
        x = x_ref[...]
        ms = jnp.mean(x * x, axis=-1, keepdims=True)
        xn_ref[...] = (x * lax.rsqrt(ms + EPS) * nw_ref[...]).astype(BF16)

    xn = xn_ref[...]
    accs = [_dot(xn, w[...]) for w in w_refs]
    o_ref[...] = epi(accs, [e[...] for e in ex_refs]).astype(o_ref.dtype)


def _norm_mm(x, nw, ws, *, epi, out_dtype, tm, tn, name, xcol=0, extras=(), w_col_offs=None):
    s = x.shape[0]
    k = nw.shape[-1]
    n = ws[0].shape[1] if w_col_offs is None else w_col_offs[1]
    offs = [0] * len(ws) if w_col_offs is None else w_col_offs[0]
    grid = (s // tm, n // tn)
    in_specs = [pl.BlockSpec((tm, k), lambda i, j: (i, xcol)),
                pl.BlockSpec((1, k), lambda i, j: (0, 0))]
    for off in offs:
        in_specs.append(pl.BlockSpec((k, tn), lambda i, j, off=off: (0, j + off)))
    for e in extras:
        in_specs.append(pl.BlockSpec((tm, e.shape[1]), lambda i, j: (i, 0)))
    return pl.pallas_call(
        functools.partial(_norm_mm_kernel, n_w=len(ws), n_ex=len(extras), epi=epi),
        grid=grid,
        in_specs=in_specs,
        out_specs=pl.BlockSpec((tm, tn), lambda i, j: (i, j)),
        out_shape=jax.ShapeDtypeStruct((s, n), out_dtype),
        scratch_shapes=[pltpu.VMEM((tm, k), BF16)],
        compiler_params=_cparams(2),
        name=name,
    )(x, nw.reshape(1, k), *ws, *extras)


def _epi_id(accs, ex):
    return accs[0]


def _epi_sigmoid(accs, ex):
    return _sigmoid(accs[0])


def _epi_swiglu(accs, ex):
    return _silu(accs[0]) * accs[1]


def _rope_chunk(c, ta, tb, tc):
    return c * ta + pltpu.roll(c, 96, 1) * tb + pltpu.roll(c, 32, 1) * tc


def _epi_rope_k(accs, ex):
    return _rope_chunk(accs[0], *ex)


def _epi_rope_q(accs, ex, *, scale):
    acc = accs[0]
    outs = []
    for h in range(acc.shape[1] // MLA_QK_PAD):
        base = h * MLA_QK_PAD
        outs.append(acc[:, base:base + LANES] * scale)
        outs.append(_rope_chunk(acc[:, base + LANES:base + 2 * LANES], *ex) * scale)
    return jnp.concatenate(outs, axis=1)


def _mm_res_kernel(a_ref, w_ref, r_ref, o_ref, *, scale):
    o_ref[...] = r_ref[...] + scale * _dot(a_ref[...], w_ref[...])


def _mm_res(a, w, res, *, scale, tm, tn, name):
    s, k = a.shape
    n = w.shape[1]
    return pl.pallas_call(
        functools.partial(_mm_res_kernel, scale=scale),
        grid=(s // tm, n // tn),
        in_specs=[pl.BlockSpec((tm, k), lambda i, j: (i, 0)),
                  pl.BlockSpec((k, tn), lambda i, j: (0, j)),
                  pl.BlockSpec((tm, tn), lambda i, j: (i, j))],
        out_specs=pl.BlockSpec((tm, tn), lambda i, j: (i, j)),
        out_shape=jax.ShapeDtypeStruct((s, n), F32),
        compiler_params=_cparams(2),
        name=name,
    )(a, w, res)


def _merge_kernel(a_ref, b_ref, c_ref, wa_ref, wb_ref, wc_ref, ga_ref, gb_ref, gc_ref, o_ref):
    ya = _dot(a_ref[...], wa_ref[...])
    yb = _dot(b_ref[...], wb_ref[...])
    yc = _dot(c_ref[...], wc_ref[...])
    o = (ga_ref[...].astype(F32) * ya + gb_ref[...].astype(F32) * yb
         + gc_ref[...].astype(F32) * yc)
    o_ref[...] = o.astype(o_ref.dtype)


def _merge(a, b, c, wa, wb, wc, gates, *, tm, tn, name):
    s = a.shape[0]
    n = wa.shape[1]
    nj = n // tn
    row = lambda arr: pl.BlockSpec((tm, arr.shape[1]), lambda i, j: (i, 0))
    col = lambda arr: pl.BlockSpec((arr.shape[0], tn), lambda i, j: (0, j))
    gate = lambda b_: pl.BlockSpec((tm, tn), lambda i, j, b_=b_: (i, j + b_ * nj))
    return pl.pallas_call(
        _merge_kernel,
        grid=(s // tm, nj),
        in_specs=[row(a), row(b), row(c), col(wa), col(wb), col(wc),
                  gate(0), gate(1), gate(2)],
        out_specs=pl.BlockSpec((tm, tn), lambda i, j: (i, j)),
        out_shape=jax.ShapeDtypeStruct((s, n), BF16),
        compiler_params=_cparams(2),
        name=name,
    )(a, b, c, wa, wb, wc, gates, gates, gates)


def _rms_kernel(x_ref, w_ref, o_ref):
    x = x_ref[...]
    ms = jnp.mean(x * x, axis=-1, keepdims=True)
    o_ref[...] = x * lax.rsqrt(ms + EPS) * w_ref[...]


def _rmsnorm(x, w, *, tm, name):
    s, d = x.shape
    return pl.pallas_call(
        _rms_kernel,
        grid=(s // tm,),
        in_specs=[pl.BlockSpec((tm, d), lambda i: (i, 0)),
                  pl.BlockSpec((1, d), lambda i: (0, 0))],
        out_specs=pl.BlockSpec((tm, d), lambda i: (i, 0)),
        out_shape=jax.ShapeDtypeStruct((s, d), F32),
        compiler_params=_cparams(1),
        name=name,
    )(x, w.reshape(1, d))


def _flash_kernel(q_ref, kn_ref, kr_ref, v_ref, o_ref, *, t):
    qb = pl.program_id(1)
    q = q_ref[...]

    def block(kb):
        off = pl.multiple_of(kb * t, t)
        k = jnp.concatenate([kn_ref[pl.ds(off, t), :], kr_ref[pl.ds(off, t), :]], axis=1)
        return _dot_nt(q, k), v_ref[pl.ds(off, t), :]

    def update(carry, s, v):
        m, l, acc = carry
        m_new = jnp.maximum(m, jnp.max(s, axis=-1, keepdims=True))
        alpha = jnp.exp(m - m_new)
        p = jnp.exp(s - m_new)
        l = alpha * l + jnp.sum(p, axis=-1, keepdims=True)
        acc = alpha * acc + _dot(p.astype(BF16), v)
        return m_new, l, acc

    def body(kb, carry):
        s, v = block(kb)
        return update(carry, s, v)

    init = (jnp.full((t, 1), -jnp.inf, F32), jnp.zeros((t, 1), F32),
            jnp.zeros((t, MLA_V), F32))
    carry = lax.fori_loop(0, qb, body, init)
    s, v = block(qb)
    row = lax.broadcasted_iota(jnp.int32, (t, t), 0) // CHUNK
    col = lax.broadcasted_iota(jnp.int32, (t, t), 1) // CHUNK
    s = jnp.where(col <= row, s, -jnp.inf)
    _, l, acc = update(carry, s, v)
    o_ref[...] = (acc / l).astype(o_ref.dtype)


def _flash(q, kv, kr, *, t, name):
    s = q.shape[0]
    return pl.pallas_call(
        functools.partial(_flash_kernel, t=t),
        grid=(MLA_HEADS, s // t),
        in_specs=[pl.BlockSpec((t, MLA_QK_PAD), lambda h, i: (i, h)),
                  pl.BlockSpec((s, MLA_NOPE), lambda h, i: (0, h)),
                  pl.BlockSpec((s, LANES), lambda h, i: (0, 0)),
                  pl.BlockSpec((s, MLA_V), lambda h, i: (0, MLA_HEADS + h))],
        out_specs=pl.BlockSpec((t, MLA_V), lambda h, i: (i, h)),
        out_shape=jax.ShapeDtypeStruct((s, MLA_HEADS * MLA_V), BF16),
        compiler_params=_cparams(2),
        name=name,
    )(q, kv, kr, kv)


def _hgrn_consts():
    c = CHUNK
    em = np.zeros((HG_LEVELS, c, c), np.float32)
    mask = np.zeros((HG_LEVELS, c, c), np.float32)
    t = np.arange(c)[:, None]
    u = np.arange(c)[None, :]
    em[0] = (u <= t)
    mask[0] = (u == t)
    for lvl in range(1, HG_LEVELS):
        m = c >> lvl
        u0 = (t // (2 * m)) * (2 * m) + m
        upper = t >= u0
        em[lvl] = np.where(upper, (u > u0) & (u <= t), (u > t) & (u <= u0))
        mask[lvl] = ((t // (2 * m)) == (u // (2 * m))) & (t % (2 * m) >= m) & (u % (2 * m) < m)
    return em.reshape(HG_LEVELS * c, c), mask


def _hgrn_kernel(q_ref, f_ref, i_ref, g_ref, lg_ref, nw_ref, em_ref, mask_ref, o_ref, st_ref,
                 *, layer, n_chunks):
    @pl.when(pl.program_id(1) == 0)
    def _():
        st_ref[...] = jnp.zeros_like(st_ref)

    lg = lg_ref[...]
    e = jnp.exp(lg - jnp.max(lg, axis=0, keepdims=True))
    p = e / jnp.sum(e, axis=0, keepdims=True)
    lb = jnp.sum(p[0:layer + 1], axis=0, keepdims=True) - p[0:1]

    em = em_ref[...]
    nw = nw_ref[...]
    for c in range(n_chunks):
        rows = slice(c * CHUNK, (c + 1) * CHUNK)
        f = lb + (1.0 - lb) * _sigmoid(f_ref[rows, :])
        k = 1.0 - f
        q = _silu(q_ref[rows, :]) * HG_D ** -0.5
        v = i_ref[rows, :].astype(BF16)
        ex = _dot(em, _split_bf16(jnp.log(f)))
        ex = ex[:, :HG_D] + ex[:, HG_D:]
        b = ex[0:CHUNK]
        b_last = b[CHUNK - 1:CHUNK, :]
        att = _dot_nt(q.astype(BF16), k.astype(BF16)) * mask_ref[0]
        for lvl in range(1, HG_LEVELS):
            w = jnp.exp(ex[lvl * CHUNK:(lvl + 1) * CHUNK])
            att = att + _dot_nt((q * w).astype(BF16), (k * w).astype(BF16)) * mask_ref[lvl]
        st = st_ref[...]
        o = _dot_nt((q * jnp.exp(b)).astype(BF16), st.astype(BF16))
        o = o + _dot(att.astype(BF16), v)
        kd = (k * jnp.exp(b_last - b)).astype(BF16)
        st_ref[...] = st * jnp.exp(b_last) + _dot_tn(v, kd)
        ms = jnp.mean(o * o, axis=-1, keepdims=True)
        o = o * lax.rsqrt(ms + EPS) * nw * _silu(g_ref[rows, :])
        o_ref[rows, :] = o.astype(o_ref.dtype)


def _hgrn(hg, lb_logits, norm_w, *, layer, n_chunks, name):
    s = hg.shape[0]
    tb = n_chunks * CHUNK
    em, mask = _hgrn_consts()
    nl = lb_logits.shape[0]
    piece = lambda b_: pl.BlockSpec((tb, HG_D), lambda h, c, b_=b_: (c, h + b_ * HG_HEADS))
    return pl.pallas_call(
        functools.partial(_hgrn_kernel, layer=layer, n_chunks=n_chunks),
        grid=(HG_HEADS, s // tb),
        in_specs=[piece(0), piece(1), piece(2), piece(3),
                  pl.BlockSpec((nl, HG_D), lambda h, c: (0, h)),
                  pl.BlockSpec((1, HG_D), lambda h, c: (0, h)),
                  pl.BlockSpec(em.shape, lambda h, c: (0, 0)),
                  pl.BlockSpec(mask.shape, lambda h, c: (0, 0, 0))],
        out_specs=pl.BlockSpec((tb, HG_D), lambda h, c: (c, h)),
        out_shape=jax.ShapeDtypeStruct((s, HG_WIDTH), BF16),
        scratch_shapes=[pltpu.VMEM((HG_D, HG_D), F32)],
        compiler_params=_cparams(2),
        name=name,
    )(hg, hg, hg, hg, lb_logits, norm_w.reshape(1, HG_WIDTH),
      jnp.asarray(em, BF16), jnp.asarray(mask, F32))


def _conv_kernel(x_ref, w_ref, b_ref, o_ref, tail_ref):
    @pl.when(pl.program_id(1) == 0)
    def _():
        tail_ref[...] = jnp.zeros_like(tail_ref)

    x = x_ref[...]
    tb = x.shape[0]
    ext = jnp.concatenate([tail_ref[...], x], axis=0)
    acc = b_ref[...] + w_ref[SSM_CONV - 1:SSM_CONV, :] * x
    for k in range(1, SSM_CONV):
        shifted = pltpu.roll(ext, k, 0)[SUBLANES:, :]
        acc = acc + w_ref[SSM_CONV - 1 - k:SSM_CONV - k, :] * shifted
    o_ref[...] = _silu(acc)
    tail_ref[...] = x[tb - SUBLANES:, :]


def _conv(zx, w, b, *, col0, tb, tc, name):
    s = zx.shape[0]
    cb0 = col0 // tc
    return pl.pallas_call(
        _conv_kernel,
        grid=(SSM_CONV_DIM // tc, s // tb),
        in_specs=[pl.BlockSpec((tb, tc), lambda c, r: (r, c + cb0)),
                  pl.BlockSpec((SSM_CONV, tc), lambda c, r: (0, c)),
                  pl.BlockSpec((1, tc), lambda c, r: (0, c))],
        out_specs=pl.BlockSpec((tb, tc), lambda c, r: (r, c)),
        out_shape=jax.ShapeDtypeStruct((s, SSM_CONV_DIM), F32),
        scratch_shapes=[pltpu.VMEM((SUBLANES, tc), F32)],
        compiler_params=_cparams(2),
        name=name,
    )(zx, w, b.reshape(1, SSM_CONV_DIM))


def _ssd_lane(l):
    return l // 2 + ODD_SLOT * (l % 2)


def _ssd_consts():
    c = CHUNK
    ltri = (np.arange(c)[None, :] <= np.arange(c)[:, None]).astype(np.float32)
    expand = np.zeros((LANES, SSM_GW), np.float32)
    for l in range(SSM_HPG):
        expand[_ssd_lane(l), l * SSM_P:(l + 1) * SSM_P] = 1.0
    expand2 = np.concatenate([expand, expand], axis=0)
    colsel = np.zeros((SSM_HPG // 2, 2 * LANES, LANES), np.float32)
    for j in range(SSM_HPG // 2):
        for e in range(2):
            for part in range(2):
                colsel[j, part * LANES + _ssd_lane(2 * j + e), e * c:(e + 1) * c] = 1.0
    eye = np.eye(c, dtype=np.float32)
    return ltri, expand2, colsel, eye


def _ssd_kernel(xs_ref, z_ref, b_ref, c_ref, dt_ref, bias_ref, alog_ref, dsk_ref, nw_ref,
                ltri_ref, ex2_ref, colsel_ref, eye_ref, o_ref, st_ref, *, n_chunks):
    @pl.when(pl.program_id(1) == 0)
    def _():
        st_ref[...] = jnp.zeros_like(st_ref)

    tb = n_chunks * CHUNK
    dt = _softplus(dt_ref[...] + bias_ref[...])
    da = dt * (-jnp.exp(alog_ref[...]))
    ltri = ltri_ref[...]
    eye = eye_ref[...]
    cums, cum_ts, decs = [], [], []
    for c in range(n_chunks):
        rows = slice(c * CHUNK, (c + 1) * CHUNK)
        cs = _dot(ltri, _split_bf16(da[rows]))
        cum = cs[:, :LANES] + cs[:, LANES:]
        cum2 = _split_bf16(cum)
        cum_t = _dot_tn(cum2[:, :LANES], eye) + _dot_tn(cum2[:, LANES:], eye)
        cums.append(cum)
        cum_ts.append(cum_t)
        decs.append(jnp.exp(cum[CHUNK - 1:CHUNK, :] - cum))
    cum_all = jnp.concatenate(cums, axis=0)
    stack = jnp.concatenate([dt, jnp.concatenate(decs, axis=0), jnp.exp(cum_all)], axis=0)
    wide = _dot(_split_bf16(stack), ex2_ref[...])
    dtx, decx, ecx = wide[:tb], wide[tb:2 * tb], wide[2 * tb:]

    xs = xs_ref[...]
    xdt = xs * dtx
    lane = lax.broadcasted_iota(jnp.int32, (CHUNK, LANES), 1)
    lo = lane < CHUNK
    tril = (lax.broadcasted_iota(jnp.int32, (CHUNK, CHUNK), 1)
            <= lax.broadcasted_iota(jnp.int32, (CHUNK, CHUNK), 0))
    st = st_ref[...]
    ys = []
    for c in range(n_chunks):
        rows = slice(c * CHUNK, (c + 1) * CHUNK)
        bm = b_ref[rows, :].astype(BF16)
        cm = c_ref[rows, :].astype(BF16)
        cb = jnp.where(tril, _dot_nt(cm, bm), 0.0)
        cb2 = jnp.concatenate([cb, cb], axis=1)
        cum2 = _split_bf16(cums[c])
        cum_t = cum_ts[c]
        rowsrc = jnp.concatenate([cum_t[0:SUBLANES], cum_t[ODD_SLOT:ODD_SLOT + SUBLANES]], axis=1)
        xdt_c = xdt[rows]
        pieces = []
        for j in range(SSM_HPG // 2):
            colb = _dot(cum2, colsel_ref[j])
            rowb = jnp.broadcast_to(rowsrc[j:j + 1, :], (CHUNK, LANES))
            m = cb2 * jnp.exp(jnp.minimum(colb - rowb, 0.0))
            xp = xdt_c[:, j * LANES:(j + 1) * LANES]
            rhs = jnp.concatenate([jnp.where(lo, xp, 0.0), jnp.where(lo, 0.0, xp)], axis=0)
            pieces.append(_dot(m.astype(BF16), rhs.astype(BF16)))
        y = jnp.concatenate(pieces, axis=1)
        y = y + _dot(cm, st.astype(BF16)) * ecx[rows]
        dlast = ecx[(c + 1) * CHUNK - 1:(c + 1) * CHUNK, :]
        st = st * dlast + _dot_tn(bm, (xdt_c * decx[rows]).astype(BF16))
        ys.append(y)
    st_ref[...] = st
    y = jnp.concatenate(ys, axis=0) + xs * dsk_ref[...]
    y = y * _silu(z_ref[...])
    ms = jnp.mean(y * y, axis=-1, keepdims=True)
    o_ref[...] = (y * lax.rsqrt(ms + EPS) * nw_ref[...]).astype(o_ref.dtype)


def _ssd(conv, zx, dt, dt_bias, a_log, d_skip, norm_w, *, n_chunks, name):
    s = conv.shape[0]
    tb = n_chunks * CHUNK
    ltri, expand2, colsel, eye = _ssd_consts()
    gw = SSM_GW // LANES
    const2 = lambda a: pl.BlockSpec(a.shape, lambda g, r: (0, 0))
    return pl.pallas_call(
        functools.partial(_ssd_kernel, n_chunks=n_chunks),
        grid=(SSM_GROUPS, s // tb),
        in_specs=[pl.BlockSpec((tb, SSM_GW), lambda g, r: (r, g)),
                  pl.BlockSpec((tb, SSM_GW), lambda g, r: (r, g)),
                  pl.BlockSpec((tb, SSM_N), lambda g, r: (r, SSM_GROUPS * gw + g)),
                  pl.BlockSpec((tb, SSM_N), lambda g, r: (r, SSM_GROUPS * (gw + 1) + g)),
                  pl.BlockSpec((tb, LANES), lambda g, r: (r, g)),
                  pl.BlockSpec((1, LANES), lambda g, r: (0, g)),
                  pl.BlockSpec((1, LANES), lambda g, r: (0, g)),
                  pl.BlockSpec((1, SSM_GW), lambda g, r: (0, g)),
                  pl.BlockSpec((1, SSM_GW), lambda g, r: (0, g)),
                  const2(ltri), const2(expand2),
                  pl.BlockSpec(colsel.shape, lambda g, r: (0, 0, 0)),
                  const2(eye)],
        out_specs=pl.BlockSpec((tb, SSM_GW), lambda g, r: (r, g)),
        out_shape=jax.ShapeDtypeStruct((s, SSM_INNER), BF16),
        scratch_shapes=[pltpu.VMEM((SSM_N, SSM_GW), F32)],
        compiler_params=_cparams(2),
        name=name,
    )(conv, zx, conv, conv, dt, dt_bias, a_log, d_skip, norm_w.reshape(1, SSM_INNER),
      jnp.asarray(ltri, BF16), jnp.asarray(expand2, BF16), jnp.asarray(colsel, BF16),
      jnp.asarray(eye, BF16))


IN_SIZES = (MLA_RANK, MLA_RANK, MLA_ROPE, HG_WIDTH, HG_WIDTH, HG_WIDTH, HG_WIDTH,
            SSM_INNER, SSM_CONV_DIM, SSM_HEADS, 3 * D_MODEL)
IN_OFFS = tuple(int(v) for v in np.cumsum((0,) + IN_SIZES))


def _dt_layout():
    idx = np.zeros((SSM_GROUPS * LANES,), np.int32)
    valid = np.zeros((SSM_GROUPS * LANES,), np.float32)
    for g in range(SSM_GROUPS):
        for l in range(SSM_HPG):
            idx[g * LANES + _ssd_lane(l)] = g * SSM_HPG + l
            valid[g * LANES + _ssd_lane(l)] = 1.0
    return idx, valid


def _rope_tables(s):
    inv = 1.0 / (ROPE_THETA ** (jnp.arange(0, MLA_ROPE, 2, dtype=F32) / MLA_ROPE))
    ang = jnp.arange(s, dtype=F32)[:, None] * inv[None, :]
    cos, sin = jnp.cos(ang), jnp.sin(ang)
    z32 = jnp.zeros_like(cos)
    z64 = jnp.concatenate([z32, z32], axis=1)
    ta = jnp.concatenate([cos, cos, z64], axis=1)
    tb = jnp.concatenate([-sin, z32, z64], axis=1)
    tc = jnp.concatenate([z32, sin, z64], axis=1)
    return ta, tb, tc


def kernel(x, ffn1_norm, ffn1_wi, ffn1_wo, mix_norm, w_in, mla_q_norm, mla_w_uq, mla_kv_norm,
           mla_w_ukv, hgrn_lb_logits, hgrn_norm, ssm_conv_w, ssm_conv_b, ssm_a_log, ssm_dt_bias,
           ssm_d, ssm_norm, w_o_mla, w_o_hgrn, w_o_ssm, w_out, ffn2_norm, ffn2_wi, ffn2_wo,
           final_norm):
    assert x.shape[0] == 1 and x.shape[2] == D_MODEL
    s = x.shape[1]
    x = x[0]
    rope = _rope_tables(s)
    dt_idx, dt_valid = _dt_layout()
    o = IN_OFFS
    qk_scale = (MLA_NOPE + MLA_ROPE) ** -0.5

    def ffn(x, norm_w, wi, wo, tag):
        wi = wi.astype(BF16)
        h = _norm_mm(x, norm_w, [wi, wi], epi=_epi_swiglu, out_dtype=BF16, tm=1024, tn=512,
                     w_col_offs=([0, D_FF // 512], D_FF), name=tag + "_up")
        return _mm_res(h, wo.astype(BF16), x, scale=0.5, tm=512, tn=512, name=tag + "_down")

    for l in range(DEPTH):
        x = ffn(x, ffn1_norm[l], ffn1_wi[l], ffn1_wo[l], f"l{l}_ffn1")

        w = w_in[l]
        nw = mix_norm[l]
        lat = _norm_mm(x, nw, [w[:, o[0]:o[2]].astype(BF16)], epi=_epi_id, out_dtype=F32,
                       tm=1024, tn=512, name=f"l{l}_in_lat")
        w_kpe = jnp.pad(w[:, o[2]:o[3]], ((0, 0), (0, LANES - MLA_ROPE))).astype(BF16)
        k_rot = _norm_mm(x, nw, [w_kpe], epi=_epi_rope_k, out_dtype=BF16, tm=1024, tn=LANES,
                         extras=rope, name=f"l{l}_in_kpe")
        hg = _norm_mm(x, nw, [w[:, o[3]:o[7]].astype(BF16)], epi=_epi_id, out_dtype=F32,
                      tm=1024, tn=512, name=f"l{l}_in_hgrn")
        zx = _norm_mm(x, nw, [w[:, o[7]:o[9]].astype(BF16)], epi=_epi_id, out_dtype=F32,
                      tm=1024, tn=512, name=f"l{l}_in_ssm")
        w_dt = (w[:, o[9]:o[10]][:, dt_idx] * dt_valid).astype(BF16)
        dt = _norm_mm(x, nw, [w_dt], epi=_epi_id, out_dtype=F32, tm=1024, tn=512,
                      name=f"l{l}_in_dt")
        gates = _norm_mm(x, nw, [w[:, o[10]:o[11]].astype(BF16)], epi=_epi_sigmoid,
                         out_dtype=BF16, tm=1024, tn=512, name=f"l{l}_in_gates")

        wq = mla_w_uq[l].reshape(MLA_RANK, MLA_HEADS, MLA_NOPE + MLA_ROPE)
        wq = jnp.pad(wq, ((0, 0), (0, 0), (0, MLA_QK_PAD - MLA_NOPE - MLA_ROPE)))
        wq = wq.reshape(MLA_RANK, MLA_HEADS * MLA_QK_PAD).astype(BF16)
        q = _norm_mm(lat, mla_q_norm[l], [wq],
                     epi=functools.partial(_epi_rope_q, scale=qk_scale), out_dtype=BF16,
                     tm=1024, tn=512, xcol=0, extras=rope, name=f"l{l}_mla_q")
        wkv = mla_w_ukv[l].reshape(MLA_RANK, MLA_HEADS, MLA_NOPE + MLA_V)
        wkv = jnp.concatenate([wkv[:, :, :MLA_NOPE].reshape(MLA_RANK, -1),
                               wkv[:, :, MLA_NOPE:].reshape(MLA_RANK, -1)], axis=1).astype(BF16)
        kv = _norm_mm(lat, mla_kv_norm[l], [wkv], epi=_epi_id, out_dtype=BF16,
                      tm=1024, tn=512, xcol=1, name=f"l{l}_mla_kv")
        y_a = _flash(q, kv, k_rot, t=512, name=f"l{l}_mla_attn")

        y_b = _hgrn(hg, hgrn_lb_logits, hgrn_norm[l], layer=l, n_chunks=4, name=f"l{l}_hgrn")

        conv = _conv(zx, ssm_conv_w[l], ssm_conv_b[l], col0=SSM_INNER, tb=256, tc=1024,
                     name=f"l{l}_ssm_conv")
        y_c = _ssd(conv, zx, dt,
                   (ssm_dt_bias[l][dt_idx] * dt_valid).reshape(1, -1),
                   (ssm_a_log[l][dt_idx] * dt_valid).reshape(1, -1),
                   jnp.repeat(ssm_d[l], SSM_P).reshape(1, -1), ssm_norm[l],
                   n_chunks=4, name=f"l{l}_ssm_scan")

        m = _merge(y_a, y_b, y_c, w_o_mla[l].astype(BF16), w_o_hgrn[l].astype(BF16),
                   w_o_ssm[l].astype(BF16), gates, tm=512, tn=256, name=f"l{l}_merge")
        x = _mm_res(m, w_out[l].astype(BF16), x, scale=1.0, tm=1024, tn=512, name=f"l{l}_out")

        x = ffn(x, ffn2_norm[l], ffn2_wi[l], ffn2_wo[l], f"l{l}_ffn2")

    return _rmsnorm(x, final_norm, tm=512, name="final_norm")[None]
```

```python
import functools

import numpy as np
import jax
import jax.numpy as jnp
from jax import lax
from jax.experimental import pallas as pl
from jax.experimental.pallas import tpu as pltpu

F32 = jnp.float32
BF16 = jnp.bfloat16

D_MODEL = 2048
DEPTH = 2
CHUNK = 64
EPS = 1e-6

MLA_HEADS = 16
MLA_RANK = 512
MLA_NOPE = 128
MLA_ROPE = 64
MLA_V = 128
MLA_QK_PAD = 256
ROPE_THETA = 10000.0
FLASH_T = 512
FLASH_UNROLL = 2

HG_HEADS = 16
HG_D = 128
HG_WIDTH = HG_HEADS * HG_D
HG_LEVELS = 7

SSM_INNER = 4096
SSM_P = 64
SSM_HEADS = 64
SSM_GROUPS = 8
SSM_HPG = SSM_HEADS // SSM_GROUPS
SSM_N = 128
SSM_GW = SSM_INNER // SSM_GROUPS
SSM_CONV = 4
SSM_CONV_DIM = SSM_INNER + 2 * SSM_GROUPS * SSM_N
ODD_SLOT = 8

D_FF = 5632

LANES = 128
SUBLANES = 8
VMEM_LIMIT = 52 * 1024 * 1024


def _cparams(n_axes):
    return pltpu.CompilerParams(
        dimension_semantics=("arbitrary",) * n_axes, vmem_limit_bytes=VMEM_LIMIT)


def _sigmoid(x):
    return jax.nn.sigmoid(x)


def _silu(x):
    return x * jax.nn.sigmoid(x)


def _softplus(x):
    return jnp.maximum(x, 0.0) + jnp.log1p(jnp.exp(-jnp.abs(x)))


def _split_bf16(x):
    hi = x.astype(BF16)
    mid = (x - hi.astype(F32)).astype(BF16)
    return jnp.concatenate([hi, mid], axis=1)


def _dot(a, b):
    return jnp.dot(a, b, preferred_element_type=F32)


def _dot_nt(a, b):
    return lax.dot_general(a, b, (((1,), (1,)), ((), ())), preferred_element_type=F32)


def _dot_tn(a, b):
    return lax.dot_general(a, b, (((0,), (0,)), ((), ())), preferred_element_type=F32)


def _norm_mm_kernel(*refs, n_w, n_ex, epi):
    x_ref, nw_ref = refs[0], refs[1]
    w_refs = refs[2:2 + n_w]
    ex_refs = refs[2 + n_w:2 + n_w + n_ex]
    o_ref = refs[2 + n_w + n_ex]
    xn_ref = refs[3 + n_w + n_ex]

    @pl.when(pl.program_id(1) == 0)
    def _():
        x = x_ref[...]
        ms = jnp.mean(x * x, axis=-1, keepdims=True)
        xn_ref[...] = (x * lax.rsqrt(ms + EPS) * nw_ref[...]).astype(BF16)

    xn = xn_ref[...]
    accs = [_dot(xn, w[...]) for w in w_refs]
    o_ref[...] = epi(accs, [e[...] for e in ex_refs]).astype(o_ref.dtype)


def _norm_mm(x, nw, ws, *, epi, out_dtype, tm, tn, name, xcol=0, extras=(), w_col_offs=None):
    s = x.shape[0]
    k = nw.shape[-1]
    n = ws[0].shape[1] if w_col_offs is None else w_col_offs[1]
    offs = [0] * len(ws) if w_col_offs is None else w_col_offs[0]
    grid = (s // tm, n // tn)
    in_specs = [pl.BlockSpec((tm, k), lambda i, j: (i, xcol)),
                pl.BlockSpec((1, k), lambda i, j: (0, 0))]
    for off in offs:
        in_specs.append(pl.BlockSpec((k, tn), lambda i, j, off=off: (0, j + off)))
    for e in extras:
        in_specs.append(pl.BlockSpec((tm, e.shape[1]), lambda i, j: (i, 0)))
    return pl.pallas_call(
        functools.partial(_norm_mm_kernel, n_w=len(ws), n_ex=len(extras), epi=epi),
        grid=grid,
        in_specs=in_specs,
        out_specs=pl.BlockSpec((tm, tn), lambda i, j: (i, j)),
        out_shape=jax.ShapeDtypeStruct((s, n), out_dtype),
        scratch_shapes=[pltpu.VMEM((tm, k), BF16)],
        compiler_params=_cparams(2),
        name=name,
    )(x, nw.reshape(1, k), *ws, *extras)


def _epi_id(accs, ex):
    return accs[0]


def _epi_sigmoid(accs, ex):
    return _sigmoid(accs[0])


def _epi_swiglu(accs, ex):
    return _silu(accs[0]) * accs[1]


def _rope_chunk(c, ta, tb, tc):
    return c * ta + pltpu.roll(c, 96, 1) * tb + pltpu.roll(c, 32, 1) * tc


def _epi_rope_k(accs, ex):
    return _rope_chunk(accs[0], *ex)


def _epi_rope_q(accs, ex, *, scale):
    acc = accs[0]
    outs = []
    for h in range(acc.shape[1] // MLA_QK_PAD):
        base = h * MLA_QK_PAD
        outs.append(acc[:, base:base + LANES] * scale)
        outs.append(_rope_chunk(acc[:, base + LANES:base + 2 * LANES], *ex) * scale)
    return jnp.concatenate(outs, axis=1)


def _mm_res_kernel(a_ref, w_ref, r_ref, o_ref, *, scale):
    o_ref[...] = r_ref[...] + scale * _dot(a_ref[...], w_ref[...])


def _mm_res(a, w, res, *, scale, tm, tn, name):
    s, k = a.shape
    n = w.shape[1]
    return pl.pallas_call(
        functools.partial(_mm_res_kernel, scale=scale),
        grid=(s // tm, n // tn),
        in_specs=[pl.BlockSpec((tm, k), lambda i, j: (i, 0)),
                  pl.BlockSpec((k, tn), lambda i, j: (0, j)),
                  pl.BlockSpec((tm, tn), lambda i, j: (i, j))],
        out_specs=pl.BlockSpec((tm, tn), lambda i, j: (i, j)),
        out_shape=jax.ShapeDtypeStruct((s, n), F32),
        compiler_params=_cparams(2),
        name=name,
    )(a, w, res)


def _merge_kernel(a_ref, b_ref, c_ref, wa_ref, wb_ref, wc_ref, ga_ref, gb_ref, gc_ref, o_ref):
    ya = _dot(a_ref[...], wa_ref[...])
    yb = _dot(b_ref[...], wb_ref[...])
    yc = _dot(c_ref[...], wc_ref[...])
    o = (ga_ref[...].astype(F32) * ya + gb_ref[...].astype(F32) * yb
         + gc_ref[...].astype(F32) * yc)
    o_ref[...] = o.astype(o_ref.dtype)


def _merge(a, b, c, wa, wb, wc, gates, *, tm, tn, name):
    s = a.shape[0]
    n = wa.shape[1]
    nj = n // tn
    row = lambda arr: pl.BlockSpec((tm, arr.shape[1]), lambda i, j: (i, 0))
    col = lambda arr: pl.BlockSpec((arr.shape[0], tn), lambda i, j: (0, j))
    gate = lambda b_: pl.BlockSpec((tm, tn), lambda i, j, b_=b_: (i, j + b_ * nj))
    return pl.pallas_call(
        _merge_kernel,
        grid=(s // tm, nj),
        in_specs=[row(a), row(b), row(c), col(wa), col(wb), col(wc),
                  gate(0), gate(1), gate(2)],
        out_specs=pl.BlockSpec((tm, tn), lambda i, j: (i, j)),
        out_shape=jax.ShapeDtypeStruct((s, n), BF16),
        compiler_params=_cparams(2),
        name=name,
    )(a, b, c, wa, wb, wc, gates, gates, gates)


def _rms_kernel(x_ref, w_ref, o_ref):
    x = x_ref[...]
    ms = jnp.mean(x * x, axis=-1, keepdims=True)
    o_ref[...] = x * lax.rsqrt(ms + EPS) * w_ref[...]


def _rmsnorm(x, w, *, tm, name):
    s, d = x.shape
    return pl.pallas_call(
        _rms_kernel,
        grid=(s // tm,),
        in_specs=[pl.BlockSpec((tm, d), lambda i: (i, 0)),
                  pl.BlockSpec((1, d), lambda i: (0, 0))],
        out_specs=pl.BlockSpec((tm, d), lambda i: (i, 0)),
        out_shape=jax.ShapeDtypeStruct((s, d), F32),
        compiler_params=_cparams(1),
        name=name,
    )(x, w.reshape(1, d))


def _flash_kernel(q_ref, kn_ref, kr_ref, v_ref, o_ref, *, t, unroll):
    qb = pl.program_id(1)
    q = q_ref[...]

    def scores(kb):
        off = pl.multiple_of(kb * t, t)
        k = jnp.concatenate([kn_ref[pl.ds(off, t), :], kr_ref[pl.ds(off, t), :]], axis=1)
        return _dot_nt(q, k)

    def update(m, l, acc, s, kb):
        v = v_ref[pl.ds(pl.multiple_of(kb * t, t), t), :]
        m_new = jnp.maximum(m, jnp.max(s, axis=-1, keepdims=True))
        alpha = jnp.exp2(m - m_new)
        p = jnp.exp2(s - m_new)
        l = alpha * l + jnp.sum(p, axis=-1, keepdims=True)
        acc = alpha * acc + _dot(p.astype(BF16), v)
        return m_new, l, acc

    def single(kb, carry):
        return update(*carry, scores(kb), kb)

    def multi(j, carry):
        ss = [scores(j * unroll + u) for u in range(unroll)]
        for u in range(unroll):
            carry = update(*carry, ss[u], j * unroll + u)
        return carry

    init = (jnp.full((t, 1), -jnp.inf, F32), jnp.zeros((t, 1), F32),
            jnp.zeros((t, MLA_V), F32))
    n_multi = qb // unroll
    carry = lax.fori_loop(0, n_multi, multi, init)
    m, l, acc = lax.fori_loop(n_multi * unroll, qb, single, carry)
    row = lax.broadcasted_iota(jnp.int32, (t, t), 0) // CHUNK
    col = lax.broadcasted_iota(jnp.int32, (t, t), 1) // CHUNK
    s = jnp.where(col <= row, scores(qb), -jnp.inf)
    _, l, acc = update(m, l, acc, s, qb)
    o_ref[...] = (acc / l).astype(o_ref.dtype)


def _flash(q, kv, kr, *, t, name):
    s = q.shape[0]
    return pl.pallas_call(
        functools.partial(_flash_kernel, t=t, unroll=FLASH_UNROLL),
        grid=(MLA_HEADS, s // t),
        in_specs=[pl.BlockSpec((t, MLA_QK_PAD), lambda h, i: (i, h)),
                  pl.BlockSpec((s, MLA_NOPE), lambda h, i: (0, h)),
                  pl.BlockSpec((s, LANES), lambda h, i: (0, 0)),
                  pl.BlockSpec((s, MLA_V), lambda h, i: (0, MLA_HEADS + h))],
        out_specs=pl.BlockSpec((t, MLA_V), lambda h, i: (i, h)),
        out_shape=jax.ShapeDtypeStruct((s, MLA_HEADS * MLA_V), BF16),
        compiler_params=_cparams(2),
        name=name,
    )(q, kv, kr, kv)


def _hgrn_consts():
    c = CHUNK
    em = np.zeros((HG_LEVELS, c, c), np.float32)
    mask = np.zeros((HG_LEVELS, c, c), np.float32)
    t = np.arange(c)[:, None]
    u = np.arange(c)[None, :]
    em[0] = (u <= t)
    mask[0] = (u == t)
    for lvl in range(1, HG_LEVELS):
        m = c >> lvl
        u0 = (t // (2 * m)) * (2 * m) + m
        upper = t >= u0
        em[lvl] = np.where(upper, (u > u0) & (u <= t), (u > t) & (u <= u0))
        mask[lvl] = ((t // (2 * m)) == (u // (2 * m))) & (t % (2 * m) >= m) & (u % (2 * m) < m)
    return em.reshape(HG_LEVELS * c, c), mask


def _hgrn_kernel(q_ref, f_ref, i_ref, g_ref, lg_ref, nw_ref, em_ref, mask_ref, o_ref, st_ref,
                 *, layer, n_chunks, n_heads):
    @pl.when(pl.program_id(1) == 0)
    def _():
        st_ref[...] = jnp.zeros_like(st_ref)

    lg = lg_ref[...]
    e = jnp.exp(lg - jnp.max(lg, axis=0, keepdims=True))
    p = e / jnp.sum(e, axis=0, keepdims=True)
    lb = jnp.sum(p[0:layer + 1], axis=0, keepdims=True) - p[0:1]

    em = em_ref[...]
    f_all = lb + (1.0 - lb) * _sigmoid(f_ref[...])
    logf_all = jnp.log(f_all)
    k_all = 1.0 - f_all
    q_all = _silu(q_ref[...]) * HG_D ** -0.5
    v_all = i_ref[...].astype(BF16)
    gate_all = nw_ref[...] * _silu(g_ref[...])
    outs = [[None] * n_heads for _ in range(n_chunks)]
    for h in range(n_heads):
        cols = slice(h * HG_D, (h + 1) * HG_D)
        logf = jnp.concatenate(
            [_split_bf16(logf_all[c * CHUNK:(c + 1) * CHUNK, cols]) for c in range(n_chunks)],
            axis=1)
        ex_all = _dot(em, logf)
        st = st_ref[h]
        for c in range(n_chunks):
            rows = slice(c * CHUNK, (c + 1) * CHUNK)
            q, k, v = q_all[rows, cols], k_all[rows, cols], v_all[rows, cols]
            ex = (ex_all[:, 2 * c * HG_D:(2 * c + 1) * HG_D]
                  + ex_all[:, (2 * c + 1) * HG_D:(2 * c + 2) * HG_D])
            b = ex[0:CHUNK]
            b_last = b[CHUNK - 1:CHUNK, :]
            att = _dot_nt(q.astype(BF16), k.astype(BF16)) * mask_ref[0]
            for lvl in range(1, HG_LEVELS):
                w = jnp.exp(ex[lvl * CHUNK:(lvl + 1) * CHUNK])
                att = att + _dot_nt((q * w).astype(BF16), (k * w).astype(BF16)) * mask_ref[lvl]
            o = _dot_nt((q * jnp.exp(b)).astype(BF16), st.astype(BF16))
            o = o + _dot(att.astype(BF16), v)
            kd = (k * jnp.exp(b_last - b)).astype(BF16)
            st = st * jnp.exp(b_last) + _dot_tn(v, kd)
            ms = jnp.mean(o * o, axis=-1, keepdims=True)
            outs[c][h] = o * lax.rsqrt(ms + EPS) * gate_all[rows, cols]
        st_ref[h] = st
    o_ref[...] = jnp.concatenate(
        [jnp.concatenate(r, axis=1) for r in outs], axis=0).astype(o_ref.dtype)


def _hgrn(hg, lb_logits, norm_w, *, col0, layer, n_chunks, n_heads, name):
    s = hg.shape[0]
    tb = n_chunks * CHUNK
    tw = n_heads * HG_D
    nhb = HG_HEADS // n_heads
    em, mask = _hgrn_consts()
    nl = lb_logits.shape[0]
    cb0 = col0 // tw
    piece = lambda b_: pl.BlockSpec((tb, tw), lambda h, c, b_=b_: (c, cb0 + h + b_ * nhb))
    return pl.pallas_call(
        functools.partial(_hgrn_kernel, layer=layer, n_chunks=n_chunks, n_heads=n_heads),
        grid=(nhb, s // tb),
        in_specs=[piece(0), piece(1), piece(2), piece(3),
                  pl.BlockSpec((nl, tw), lambda h, c: (0, h)),
                  pl.BlockSpec((1, tw), lambda h, c: (0, h)),
                  pl.BlockSpec(em.shape, lambda h, c: (0, 0)),
                  pl.BlockSpec(mask.shape, lambda h, c: (0, 0, 0))],
        out_specs=pl.BlockSpec((tb, tw), lambda h, c: (c, h)),
        out_shape=jax.ShapeDtypeStruct((s, HG_WIDTH), BF16),
        scratch_shapes=[pltpu.VMEM((n_heads, HG_D, HG_D), F32)],
        compiler_params=_cparams(2),
        name=name,
    )(hg, hg, hg, hg, lb_logits, norm_w.reshape(1, HG_WIDTH),
      jnp.asarray(em, BF16), jnp.asarray(mask, F32))


def _conv_kernel(x_ref, w_ref, b_ref, o_ref, tail_ref):
    @pl.when(pl.program_id(1) == 0)
    def _():
        tail_ref[...] = jnp.zeros_like(tail_ref)

    x = x_ref[...]
    tb = x.shape[0]
    ext = jnp.concatenate([tail_ref[...], x], axis=0)
    acc = b_ref[...] + w_ref[SSM_CONV - 1:SSM_CONV, :] * x
    for k in range(1, SSM_CONV):
        shifted = pltpu.roll(ext, k, 0)[SUBLANES:, :]
        acc = acc + w_ref[SSM_CONV - 1 - k:SSM_CONV - k, :] * shifted
    o_ref[...] = _silu(acc)
    tail_ref[...] = x[tb - SUBLANES:, :]


def _conv(zx, w, b, *, col0, tb, tc, name):
    s = zx.shape[0]
    cb0 = col0 // tc
    return pl.pallas_call(
        _conv_kernel,
        grid=(SSM_CONV_DIM // tc, s // tb),
        in_specs=[pl.BlockSpec((tb, tc), lambda c, r: (r, c + cb0)),
                  pl.BlockSpec((SSM_CONV, tc), lambda c, r: (0, c)),
                  pl.BlockSpec((1, tc), lambda c, r: (0, c))],
        out_specs=pl.BlockSpec((tb, tc), lambda c, r: (r, c)),
        out_shape=jax.ShapeDtypeStruct((s, SSM_CONV_DIM), F32),
        scratch_shapes=[pltpu.VMEM((SUBLANES, tc), F32)],
        compiler_params=_cparams(2),
        name=name,
    )(zx, w, b.reshape(1, SSM_CONV_DIM))


def _ssd_lane(l):
    return l // 2 + ODD_SLOT * (l % 2)


def _ssd_consts():
    c = CHUNK
    ltri = (np.arange(c)[None, :] <= np.arange(c)[:, None]).astype(np.float32)
    expand = np.zeros((LANES, SSM_GW), np.float32)
    for l in range(SSM_HPG):
        expand[_ssd_lane(l), l * SSM_P:(l + 1) * SSM_P] = 1.0
    expand2 = np.concatenate([expand, expand], axis=0)
    colsel = np.zeros((SSM_HPG // 2, 2 * LANES, LANES), np.float32)
    for j in range(SSM_HPG // 2):
        for e in range(2):
            for part in range(2):
                colsel[j, part * LANES + _ssd_lane(2 * j + e), e * c:(e + 1) * c] = 1.0
    eye = np.eye(c, dtype=np.float32)
    return ltri, expand2, colsel, eye


def _ssd_kernel(xs_ref, z_ref, b_ref, c_ref, dt_ref, bias_ref, alog_ref, dsk_ref, nw_ref,
                ltri_ref, ex2_ref, colsel_ref, eye_ref, o_ref, st_ref, *, n_chunks):
    @pl.when(pl.program_id(1) == 0)
    def _():
        st_ref[...] = jnp.zeros_like(st_ref)

    tb = n_chunks * CHUNK
    dt = _softplus(dt_ref[...] + bias_ref[...])
    da = dt * (-jnp.exp(alog_ref[...]))
    ltri = ltri_ref[...]
    eye = eye_ref[...]
    cums, cum_ts, decs = [], [], []
    for c in range(n_chunks):
        rows = slice(c * CHUNK, (c + 1) * CHUNK)
        cs = _dot(ltri, _split_bf16(da[rows]))
        cum = cs[:, :LANES] + cs[:, LANES:]
        cum2 = _split_bf16(cum)
        cum_t = _dot_tn(cum2[:, :LANES], eye) + _dot_tn(cum2[:, LANES:], eye)
        cums.append(cum)
        cum_ts.append(cum_t)
        decs.append(jnp.exp(cum[CHUNK - 1:CHUNK, :] - cum))
    cum_all = jnp.concatenate(cums, axis=0)
    stack = jnp.concatenate([dt, jnp.concatenate(decs, axis=0), jnp.exp(cum_all)], axis=0)
    wide = _dot(_split_bf16(stack), ex2_ref[...])
    dtx, decx, ecx = wide[:tb], wide[tb:2 * tb], wide[2 * tb:]

    xs = xs_ref[...]
    xdt = xs * dtx
    lane = lax.broadcasted_iota(jnp.int32, (CHUNK, LANES), 1)
    lo = lane < CHUNK
    tril = (lax.broadcasted_iota(jnp.int32, (CHUNK, CHUNK), 1)
            <= lax.broadcasted_iota(jnp.int32, (CHUNK, CHUNK), 0))
    st = st_ref[...]
    ys = []
    for c in range(n_chunks):
        rows = slice(c * CHUNK, (c + 1) * CHUNK)
        bm = b_ref[rows, :].astype(BF16)
        cm = c_ref[rows, :].astype(BF16)
        cb = jnp.where(tril, _dot_nt(cm, bm), 0.0)
        cb2 = jnp.concatenate([cb, cb], axis=1)
        cum2 = _split_bf16(cums[c])
        cum_t = cum_ts[c]
        rowsrc = jnp.concatenate([cum_t[0:SUBLANES], cum_t[ODD_SLOT:ODD_SLOT + SUBLANES]], axis=1)
        xdt_c = xdt[rows]
        pieces = []
        for j in range(SSM_HPG // 2):
            colb = _dot(cum2, colsel_ref[j])
            rowb = jnp.broadcast_to(rowsrc[j:j + 1, :], (CHUNK, LANES))
            m = cb2 * jnp.exp(jnp.minimum(colb - rowb, 0.0))
            xp = xdt_c[:, j * LANES:(j + 1) * LANES]
            rhs = jnp.concatenate([jnp.where(lo, xp, 0.0), jnp.where(lo, 0.0, xp)], axis=0)
            pieces.append(_dot(m.astype(BF16), rhs.astype(BF16)))
        y = jnp.concatenate(pieces, axis=1)
        y = y + _dot(cm, st.astype(BF16)) * ecx[rows]
        dlast = ecx[(c + 1) * CHUNK - 1:(c + 1) * CHUNK, :]
        st = st * dlast + _dot_tn(bm, (xdt_c * decx[rows]).astype(BF16))
        ys.append(y)
    st_ref[...] = st
    y = jnp.concatenate(ys, axis=0) + xs * dsk_ref[...]
    y = y * _silu(z_ref[...])
    ms = jnp.mean(y * y, axis=-1, keepdims=True)
    o_ref[...] = (y * lax.rsqrt(ms + EPS) * nw_ref[...]).astype(o_ref.dtype)


def _ssd(conv, proj, z_col0, dt_col0, dt_bias, a_log, d_skip, norm_w, *, n_chunks, name):
    s = conv.shape[0]
    tb = n_chunks * CHUNK
    ltri, expand2, colsel, eye = _ssd_consts()
    gw = SSM_GW // LANES
    const2 = lambda a: pl.BlockSpec(a.shape, lambda g, r: (0, 0))
    return pl.pallas_call(
        functools.partial(_ssd_kernel, n_chunks=n_chunks),
        grid=(SSM_GROUPS, s // tb),
        in_specs=[pl.BlockSpec((tb, SSM_GW), lambda g, r: (r, g)),
                  pl.BlockSpec((tb, SSM_GW), lambda g, r: (r, z_col0 // SSM_GW + g)),
                  pl.BlockSpec((tb, SSM_N), lambda g, r: (r, SSM_GROUPS * gw + g)),
                  pl.BlockSpec((tb, SSM_N), lambda g, r: (r, SSM_GROUPS * (gw + 1) + g)),
                  pl.BlockSpec((tb, LANES), lambda g, r: (r, dt_col0 // LANES + g)),
                  pl.BlockSpec((1, LANES), lambda g, r: (0, g)),
                  pl.BlockSpec((1, LANES), lambda g, r: (0, g)),
                  pl.BlockSpec((1, SSM_GW), lambda g, r: (0, g)),
                  pl.BlockSpec((1, SSM_GW), lambda g, r: (0, g)),
                  const2(ltri), const2(expand2),
                  pl.BlockSpec(colsel.shape, lambda g, r: (0, 0, 0)),
                  const2(eye)],
        out_specs=pl.BlockSpec((tb, SSM_GW), lambda g, r: (r, g)),
        out_shape=jax.ShapeDtypeStruct((s, SSM_INNER), BF16),
        scratch_shapes=[pltpu.VMEM((SSM_N, SSM_GW), F32)],
        compiler_params=_cparams(2),
        name=name,
    )(conv, proj, conv, conv, proj, dt_bias, a_log, d_skip, norm_w.reshape(1, SSM_INNER),
      jnp.asarray(ltri, BF16), jnp.asarray(expand2, BF16), jnp.asarray(colsel, BF16),
      jnp.asarray(eye, BF16))


IN_SIZES = (MLA_RANK, MLA_RANK, MLA_ROPE, HG_WIDTH, HG_WIDTH, HG_WIDTH, HG_WIDTH,
            SSM_INNER, SSM_CONV_DIM, SSM_HEADS, 3 * D_MODEL)
IN_OFFS = tuple(int(v) for v in np.cumsum((0,) + IN_SIZES))
PROJ_HG = 2 * MLA_RANK
PROJ_Z = PROJ_HG + 4 * HG_WIDTH
PROJ_XBC = PROJ_Z + SSM_INNER
PROJ_DT = PROJ_XBC + SSM_CONV_DIM


def _dt_layout():
    idx = np.zeros((SSM_GROUPS * LANES,), np.int32)
    valid = np.zeros((SSM_GROUPS * LANES,), np.float32)
    for g in range(SSM_GROUPS):
        for l in range(SSM_HPG):
            idx[g * LANES + _ssd_lane(l)] = g * SSM_HPG + l
            valid[g * LANES + _ssd_lane(l)] = 1.0
    return idx, valid


def _rope_tables(s):
    inv = 1.0 / (ROPE_THETA ** (jnp.arange(0, MLA_ROPE, 2, dtype=F32) / MLA_ROPE))
    ang = jnp.arange(s, dtype=F32)[:, None] * inv[None, :]
    cos, sin = jnp.cos(ang), jnp.sin(ang)
    z32 = jnp.zeros_like(cos)
    z64 = jnp.concatenate([z32, z32], axis=1)
    ta = jnp.concatenate([cos, cos, z64], axis=1)
    tb = jnp.concatenate([-sin, z32, z64], axis=1)
    tc = jnp.concatenate([z32, sin, z64], axis=1)
    return ta, tb, tc


def kernel(x, ffn1_norm, ffn1_wi, ffn1_wo, mix_norm, w_in, mla_q_norm, mla_w_uq, mla_kv_norm,
           mla_w_ukv, hgrn_lb_logits, hgrn_norm, ssm_conv_w, ssm_conv_b, ssm_a_log, ssm_dt_bias,
           ssm_d, ssm_norm, w_o_mla, w_o_hgrn, w_o_ssm, w_out, ffn2_norm, ffn2_wi, ffn2_wo,
           final_norm):
    assert x.shape[0] == 1 and x.shape[2] == D_MODEL
    s = x.shape[1]
    x = x[0]
    rope = _rope_tables(s)
    dt_idx, dt_valid = _dt_layout()
    o = IN_OFFS
    qk_scale = (MLA_NOPE + MLA_ROPE) ** -0.5 * float(np.log2(np.e))

    def ffn(x, norm_w, wi, wo, tag):
        wi = wi.astype(BF16)
        h = _norm_mm(x, norm_w, [wi, wi], epi=_epi_swiglu, out_dtype=BF16, tm=1024, tn=512,
                     w_col_offs=([0, D_FF // 512], D_FF), name=tag + "_up")
        return _mm_res(h, wo.astype(BF16), x, scale=0.5, tm=512, tn=512, name=tag + "_down")

    for l in range(DEPTH):
        x = ffn(x, ffn1_norm[l], ffn1_wi[l], ffn1_wo[l], f"l{l}_ffn1")

        wb = w_in[l].astype(BF16)
        nw = mix_norm[l]
        w_dt = wb[:, o[9]:o[10]][:, dt_idx] * jnp.asarray(dt_valid, BF16)
        w_main = jnp.concatenate([wb[:, o[0]:o[2]], wb[:, o[3]:o[9]], w_dt], axis=1)
        proj = _norm_mm(x, nw, [w_main], epi=_epi_id, out_dtype=F32, tm=1024, tn=512,
                        name=f"l{l}_in_main")
        w_kpe = jnp.pad(wb[:, o[2]:o[3]], ((0, 0), (0, LANES - MLA_ROPE)))
        k_rot = _norm_mm(x, nw, [w_kpe], epi=_epi_rope_k, out_dtype=BF16, tm=1024, tn=LANES,
                         extras=rope, name=f"l{l}_in_kpe")
        gates = _norm_mm(x, nw, [wb[:, o[10]:o[11]]], epi=_epi_sigmoid,
                         out_dtype=BF16, tm=1024, tn=512, name=f"l{l}_in_gates")

        wq = mla_w_uq[l].reshape(MLA_RANK, MLA_HEADS, MLA_NOPE + MLA_ROPE)
        wq = jnp.pad(wq, ((0, 0), (0, 0), (0, MLA_QK_PAD - MLA_NOPE - MLA_ROPE)))
        wq = wq.reshape(MLA_RANK, MLA_HEADS * MLA_QK_PAD).astype(BF16)
        q = _norm_mm(proj, mla_q_norm[l], [wq],
                     epi=functools.partial(_epi_rope_q, scale=qk_scale), out_dtype=BF16,
                     tm=1024, tn=512, xcol=0, extras=rope, name=f"l{l}_mla_q")
        wkv = mla_w_ukv[l].reshape(MLA_RANK, MLA_HEADS, MLA_NOPE + MLA_V)
        wkv = jnp.concatenate([wkv[:, :, :MLA_NOPE].reshape(MLA_RANK, -1),
                               wkv[:, :, MLA_NOPE:].reshape(MLA_RANK, -1)], axis=1).astype(BF16)
        kv = _norm_mm(proj, mla_kv_norm[l], [wkv], epi=_epi_id, out_dtype=BF16,
                      tm=1024, tn=512, xcol=1, name=f"l{l}_mla_kv")
        y_a = _flash(q, kv, k_rot, t=FLASH_T, name=f"l{l}_mla_attn")

        y_b = _hgrn(proj, hgrn_lb_logits, hgrn_norm[l], col0=PROJ_HG, layer=l, n_chunks=4,
                    n_heads=4, name=f"l{l}_hgrn")

        conv = _conv(proj, ssm_conv_w[l], ssm_conv_b[l], col0=PROJ_XBC, tb=256, tc=1024,
                     name=f"l{l}_ssm_conv")
        y_c = _ssd(conv, proj, PROJ_Z, PROJ_DT,
                   (ssm_dt_bias[l][dt_idx] * dt_valid).reshape(1, -1),
                   (ssm_a_log[l][dt_idx] * dt_valid).reshape(1, -1),
                   jnp.repeat(ssm_d[l], SSM_P).reshape(1, -1), ssm_norm[l],
                   n_chunks=4, name=f"l{l}_ssm_scan")

        m = _merge(y_a, y_b, y_c, w_o_mla[l].astype(BF16), w_o_hgrn[l].astype(BF16),
                   w_o_ssm[l].astype(BF16), gates, tm=512, tn=256, name=f"l{l}_merge")
        x = _mm_res(m, w_out[l].astype(BF16), x, scale=1.0, tm=1024, tn=512, name=f"l{l}_out")

        x = ffn(x, ffn2_norm[l], ffn2_wi[l], ffn2_wo[l], f"l{l}_ffn2")

    return _rmsnorm(x, final_norm, tm=512, name="final_norm")[None]
```

```python
import functools

import numpy as np
import jax
import jax.numpy as jnp
from jax import lax
from jax.experimental import pallas as pl
from jax.experimental.pallas import tpu as pltpu

F32 = jnp.float32
BF16 = jnp.bfloat16

D_MODEL = 2048
DEPTH = 2
CHUNK = 64
EPS = 1e-6

MLA_HEADS = 16
MLA_RANK = 512
MLA_NOPE = 128
MLA_ROPE = 64
MLA_V = 128
MLA_QK_PAD = 256
ROPE_THETA = 10000.0
FLASH_T = 512

HG_HEADS = 16
HG_D = 128
HG_WIDTH = HG_HEADS * HG_D
HG_LEVELS = 7

SSM_INNER = 4096
SSM_P = 64
SSM_HEADS = 64
SSM_GROUPS = 8
SSM_HPG = SSM_HEADS // SSM_GROUPS
SSM_N = 128
SSM_GW = SSM_INNER // SSM_GROUPS
SSM_CONV = 4
SSM_CONV_DIM = SSM_INNER + 2 * SSM_GROUPS * SSM_N
ODD_SLOT = 8

D_FF = 5632

LANES = 128
SUBLANES = 8
VMEM_LIMIT = 52 * 1024 * 1024


def _cparams(n_axes):
    return pltpu.CompilerParams(
        dimension_semantics=("arbitrary",) * n_axes, vmem_limit_bytes=VMEM_LIMIT)


def _sigmoid(x):
    return jax.nn.sigmoid(x)


def _silu(x):
    return x * jax.nn.sigmoid(x)


def _softplus(x):
    return jnp.maximum(x, 0.0) + jnp.log1p(jnp.exp(-jnp.abs(x)))


def _split_bf16(x):
    hi = x.astype(BF16)
    mid = (x - hi.astype(F32)).astype(BF16)
    return jnp.concatenate([hi, mid], axis=1)


def _dot(a, b):
    return jnp.dot(a, b, preferred_element_type=F32)


def _dot_nt(a, b):
    return lax.dot_general(a, b, (((1,), (1,)), ((), ())), preferred_element_type=F32)


def _dot_tn(a, b):
    return lax.dot_general(a, b, (((0,), (0,)), ((), ())), preferred_element_type=F32)


def _norm_mm_kernel(*refs, n_w, n_ex, epi):
    x_ref, nw_ref = refs[0], refs[1]
    w_refs = refs[2:2 + n_w]
    ex_refs = refs[2 + n_w:2 + n_w + n_ex]
    o_ref = refs[2 + n_w + n_ex]
    xn_ref = refs[3 + n_w + n_ex]

    @pl.when(pl.program_id(1) == 0)
    def _():
        x = x_ref[...]
        ms = jnp.mean(x * x, axis=-1, keepdims=True)
        xn_ref[...] = (x * lax.rsqrt(ms + EPS) * nw_ref[...]).astype(BF16)

    xn = xn_ref[...]
    accs = [_dot(xn, w[...].astype(BF16)) for w in w_refs]
    o_ref[...] = epi(accs, [e[...] for e in ex_refs]).astype(o_ref.dtype)


def _w_spec(w, k, tn, layer, col0):
    off = col0 // tn
    if w.ndim == 3:
        return pl.BlockSpec((None, k, tn), lambda i, j: (layer, 0, j + off))
    return pl.BlockSpec((k, tn), lambda i, j: (0, j + off))


def _norm_mm(x, nw, ws, *, n, epi, out_dtype, tm, tn, name, xcol=0, extras=()):
    s = x.shape[0]
    k = nw.shape[-1]
    grid = (s // tm, n // tn)
    in_specs = [pl.BlockSpec((tm, k), lambda i, j: (i, xcol)),
                pl.BlockSpec((1, k), lambda i, j: (0, 0))]
    for w, layer, col0 in ws:
        in_specs.append(_w_spec(w, k, tn, layer, col0))
    for e in extras:
        in_specs.append(pl.BlockSpec((tm, e.shape[1]), lambda i, j: (i, 0)))
    return pl.pallas_call(
        functools.partial(_norm_mm_kernel, n_w=len(ws), n_ex=len(extras), epi=epi),
        grid=grid,
        in_specs=in_specs,
        out_specs=pl.BlockSpec((tm, tn), lambda i, j: (i, j)),
        out_shape=jax.ShapeDtypeStruct((s, n), out_dtype),
        scratch_shapes=[pltpu.VMEM((tm, k), BF16)],
        compiler_params=_cparams(2),
        name=name,
    )(x, nw.reshape(1, k), *[w for w, _, _ in ws], *extras)


def _epi_id(accs, ex):
    return accs[0]


def _epi_sigmoid(accs, ex):
    return _sigmoid(accs[0])


def _epi_swiglu(accs, ex):
    return _silu(accs[0]) * accs[1]


def _rope_chunk(c, ta, tb, tc):
    return c * ta + pltpu.roll(c, 96, 1) * tb + pltpu.roll(c, 32, 1) * tc


def _epi_rope_k(accs, ex):
    return _rope_chunk(accs[0], *ex)


def _epi_rope_q(accs, ex, *, scale):
    acc = accs[0]
    outs = []
    for h in range(acc.shape[1] // MLA_QK_PAD):
        base = h * MLA_QK_PAD
        outs.append(acc[:, base:base + LANES] * scale)
        outs.append(_rope_chunk(acc[:, base + LANES:base + 2 * LANES], *ex) * scale)
    return jnp.concatenate(outs, axis=1)


def _mm_res_kernel(a_ref, w_ref, r_ref, o_ref, *, scale):
    o_ref[...] = r_ref[...] + scale * _dot(a_ref[...], w_ref[...].astype(BF16))


def _mm_res(a, w, layer, res, *, scale, tm, tn, name):
    s, k = a.shape
    n = w.shape[-1]
    return pl.pallas_call(
        functools.partial(_mm_res_kernel, scale=scale),
        grid=(s // tm, n // tn),
        in_specs=[pl.BlockSpec((tm, k), lambda i, j: (i, 0)),
                  _w_spec(w, k, tn, layer, 0),
                  pl.BlockSpec((tm, tn), lambda i, j: (i, j))],
        out_specs=pl.BlockSpec((tm, tn), lambda i, j: (i, j)),
        out_shape=jax.ShapeDtypeStruct((s, n), F32),
        compiler_params=_cparams(2),
        name=name,
    )(a, w, res)


def _merge_kernel(a_ref, b_ref, c_ref, wa_ref, wb_ref, wc_ref, ga_ref, gb_ref, gc_ref, o_ref):
    ya = _dot(a_ref[...], wa_ref[...].astype(BF16))
    yb = _dot(b_ref[...], wb_ref[...].astype(BF16))
    yc = _dot(c_ref[...], wc_ref[...].astype(BF16))
    o = (ga_ref[...].astype(F32) * ya + gb_ref[...].astype(F32) * yb
         + gc_ref[...].astype(F32) * yc)
    o_ref[...] = o.astype(o_ref.dtype)


def _merge(a, b, c, wa, wb, wc, layer, gates, *, tm, tn, name):
    s = a.shape[0]
    n = wa.shape[-1]
    nj = n // tn
    row = lambda arr: pl.BlockSpec((tm, arr.shape[1]), lambda i, j: (i, 0))
    col = lambda arr: _w_spec(arr, arr.shape[-2], tn, layer, 0)
    gate = lambda b_: pl.BlockSpec((tm, tn), lambda i, j, b_=b_: (i, j + b_ * nj))
    return pl.pallas_call(
        _merge_kernel,
        grid=(s // tm, nj),
        in_specs=[row(a), row(b), row(c), col(wa), col(wb), col(wc),
                  gate(0), gate(1), gate(2)],
        out_specs=pl.BlockSpec((tm, tn), lambda i, j: (i, j)),
        out_shape=jax.ShapeDtypeStruct((s, n), BF16),
        compiler_params=_cparams(2),
        name=name,
    )(a, b, c, wa, wb, wc, gates, gates, gates)


def _rms_kernel(x_ref, w_ref, o_ref):
    x = x_ref[...]
    ms = jnp.mean(x * x, axis=-1, keepdims=True)
    o_ref[...] = x * lax.rsqrt(ms + EPS) * w_ref[...]


def _rmsnorm(x, w, *, tm, name):
    s, d = x.shape
    return pl.pallas_call(
        _rms_kernel,
        grid=(s // tm,),
        in_specs=[pl.BlockSpec((tm, d), lambda i: (i, 0)),
                  pl.BlockSpec((1, d), lambda i: (0, 0))],
        out_specs=pl.BlockSpec((tm, d), lambda i: (i, 0)),
        out_shape=jax.ShapeDtypeStruct((s, d), F32),
        compiler_params=_cparams(1),
        name=name,
    )(x, w.reshape(1, d))


def _flash_kernel(q_ref, kn_ref, kr_ref, v_ref, o_ref, *, t):
    qb = pl.program_id(1)
    q = q_ref[...]

    def scores(kb):
        off = pl.multiple_of(kb * t, t)
        k = jnp.concatenate([kn_ref[pl.ds(off, t), :], kr_ref[pl.ds(off, t), :]], axis=1)
        return _dot_nt(q, k)

    def update(m, l, acc, s, kb):
        v = v_ref[pl.ds(pl.multiple_of(kb * t, t), t), :]
        m_new = jnp.maximum(m, jnp.max(s, axis=-1, keepdims=True))
        alpha = jnp.exp2(m - m_new)
        p = jnp.exp2(s - m_new)
        l = alpha * l + jnp.sum(p, axis=-1, keepdims=True)
        acc = alpha * acc + _dot(p.astype(BF16), v)
        return m_new, l, acc

    def multi(j, carry):
        s0, s1 = scores(2 * j), scores(2 * j + 1)
        return update(*update(*carry, s0, 2 * j), s1, 2 * j + 1)

    init = (jnp.full((t, 1), -jnp.inf, F32), jnp.zeros((t, 1), F32),
            jnp.zeros((t, MLA_V), F32))
    carry = lax.fori_loop(0, qb // 2, multi, init)

    def diag_scores():
        row = lax.broadcasted_iota(jnp.int32, (t, t), 0) // CHUNK
        col = lax.broadcasted_iota(jnp.int32, (t, t), 1) // CHUNK
        return jnp.where(col <= row, scores(qb), -jnp.inf)

    def tail_pair(carry):
        s0, s1 = scores(qb - 1), diag_scores()
        return update(*update(*carry, s0, qb - 1), s1, qb)

    def tail_single(carry):
        return update(*carry, diag_scores(), qb)

    _, l, acc = lax.cond(qb % 2 == 1, tail_pair, tail_single, carry)
    o_ref[...] = (acc / l).astype(o_ref.dtype)


def _flash(q, kv, kr, *, t, name):
    s = q.shape[0]
    return pl.pallas_call(
        functools.partial(_flash_kernel, t=t),
        grid=(MLA_HEADS, s // t),
        in_specs=[pl.BlockSpec((t, MLA_QK_PAD), lambda h, i: (i, h)),
                  pl.BlockSpec((s, MLA_NOPE), lambda h, i: (0, h)),
                  pl.BlockSpec((s, LANES), lambda h, i: (0, 0)),
                  pl.BlockSpec((s, MLA_V), lambda h, i: (0, MLA_HEADS + h))],
        out_specs=pl.BlockSpec((t, MLA_V), lambda h, i: (i, h)),
        out_shape=jax.ShapeDtypeStruct((s, MLA_HEADS * MLA_V), BF16),
        compiler_params=_cparams(2),
        name=name,
    )(q, kv, kr, kv)


def _hgrn_consts():
    c = CHUNK
    t = np.arange(c)[:, None]
    u = np.arange(c)[None, :]
    ltri = (u <= t).astype(np.float32)
    mask = np.zeros((HG_LEVELS, c, c), np.float32)
    mask[0] = (u == t)
    for lvl in range(1, HG_LEVELS):
        m = c >> lvl
        mask[lvl] = ((t // (2 * m)) == (u // (2 * m))) & (t % (2 * m) >= m) & (u % (2 * m) < m)
    return ltri, mask


def _row_masks():
    row = lax.broadcasted_iota(jnp.int32, (CHUNK, HG_D), 0)
    return row % 4 == 0, row % 4 == 1, row % 4 == 2, row % 2 == 0


def _level_refs(b, row_masks):
    c = CHUNK
    q0, q1, q2, even = row_masks
    refs = []
    for m in (32, 16, 8):
        pieces = [jnp.broadcast_to(b[u0:u0 + 1, :], (2 * m, HG_D)) for u0 in range(m, c, 2 * m)]
        refs.append(jnp.concatenate(pieces, axis=0) if len(pieces) > 1 else pieces[0])
    b3 = b.reshape(c // SUBLANES, SUBLANES, HG_D)
    refs.append(jnp.broadcast_to(b3[:, 4:5, :], b3.shape).reshape(c, HG_D))
    up1 = pltpu.roll(b, c - 1, 0)
    up2 = pltpu.roll(b, c - 2, 0)
    dn1 = pltpu.roll(b, 1, 0)
    refs.append(jnp.where(q0, up2, jnp.where(q1, up1, jnp.where(q2, b, dn1))))
    refs.append(jnp.where(even, up1, b))
    return refs


def _neg_abs(x):
    bits = lax.bitcast_convert_type(x, jnp.uint32) | jnp.uint32(0x80000000)
    return lax.bitcast_convert_type(bits, F32)


def _hgrn_kernel(q_ref, f_ref, i_ref, g_ref, lg_ref, nw_ref, ltri_ref, mask_ref, o_ref, st_ref,
                 *, layer, n_chunks, n_heads):
    @pl.when(pl.program_id(1) == 0)
    def _():
        st_ref[...] = jnp.zeros_like(st_ref)

    lg = lg_ref[...]
    e = jnp.exp(lg - jnp.max(lg, axis=0, keepdims=True))
    p = e / jnp.sum(e, axis=0, keepdims=True)
    lb = jnp.sum(p[0:layer + 1], axis=0, keepdims=True) - p[0:1]

    ltri = ltri_ref[...]
    row_masks = _row_masks()
    f_all = lb + (1.0 - lb) * _sigmoid(f_ref[...])
    logf_all = jnp.log(f_all)
    k_all = 1.0 - f_all
    q_all = _silu(q_ref[...]) * HG_D ** -0.5
    v_all = i_ref[...].astype(BF16)
    gate_all = nw_ref[...] * _silu(g_ref[...])
    outs = [[None] * n_heads for _ in range(n_chunks)]
    for h in range(n_heads):
        cols = slice(h * HG_D, (h + 1) * HG_D)
        logf = jnp.concatenate(
            [_split_bf16(logf_all[c * CHUNK:(c + 1) * CHUNK, cols]) for c in range(n_chunks)],
            axis=1)
        b_all = _dot(ltri, logf)
        st = st_ref[h]
        for c in range(n_chunks):
            rows = slice(c * CHUNK, (c + 1) * CHUNK)
            q, k, v = q_all[rows, cols], k_all[rows, cols], v_all[rows, cols]
            b = (b_all[:, 2 * c * HG_D:(2 * c + 1) * HG_D]
                 + b_all[:, (2 * c + 1) * HG_D:(2 * c + 2) * HG_D])
            ex = [None] + [_neg_abs(b - r) for r in _level_refs(b, row_masks)]
            b_last = b[CHUNK - 1:CHUNK, :]
            att = _dot_nt(q.astype(BF16), k.astype(BF16)) * mask_ref[0]
            for lvl in range(1, HG_LEVELS):
                w = jnp.exp(ex[lvl])
                att = att + _dot_nt((q * w).astype(BF16), (k * w).astype(BF16)) * mask_ref[lvl]
            o = _dot_nt((q * jnp.exp(b)).astype(BF16), st.astype(BF16))
            o = o + _dot(att.astype(BF16), v)
            kd = (k * jnp.exp(b_last - b)).astype(BF16)
            st = st * jnp.exp(b_last) + _dot_tn(v, kd)
            ms = jnp.mean(o * o, axis=-1, keepdims=True)
            outs[c][h] = o * lax.rsqrt(ms + EPS) * gate_all[rows, cols]
        st_ref[h] = st
    o_ref[...] = jnp.concatenate(
        [jnp.concatenate(r, axis=1) for r in outs], axis=0).astype(o_ref.dtype)


def _hgrn(hg, lb_logits, norm_w, *, col0, layer, n_chunks, n_heads, name):
    s = hg.shape[0]
    tb = n_chunks * CHUNK
    tw = n_heads * HG_D
    nhb = HG_HEADS // n_heads
    ltri, mask = _hgrn_consts()
    nl = lb_logits.shape[0]
    cb0 = col0 // tw
    piece = lambda b_: pl.BlockSpec((tb, tw), lambda h, c, b_=b_: (c, cb0 + h + b_ * nhb))
    return pl.pallas_call(
        functools.partial(_hgrn_kernel, layer=layer, n_chunks=n_chunks, n_heads=n_heads),
        grid=(nhb, s // tb),
        in_specs=[piece(0), piece(1), piece(2), piece(3),
                  pl.BlockSpec((nl, tw), lambda h, c: (0, h)),
                  pl.BlockSpec((1, tw), lambda h, c: (0, h)),
                  pl.BlockSpec(ltri.shape, lambda h, c: (0, 0)),
                  pl.BlockSpec(mask.shape, lambda h, c: (0, 0, 0))],
        out_specs=pl.BlockSpec((tb, tw), lambda h, c: (c, h)),
        out_shape=jax.ShapeDtypeStruct((s, HG_WIDTH), BF16),
        scratch_shapes=[pltpu.VMEM((n_heads, HG_D, HG_D), F32)],
        compiler_params=_cparams(2),
        name=name,
    )(hg, hg, hg, hg, lb_logits, norm_w.reshape(1, HG_WIDTH),
      jnp.asarray(ltri, BF16), jnp.asarray(mask, F32))


def _conv_kernel(x_ref, w_ref, b_ref, o_ref, tail_ref):
    @pl.when(pl.program_id(1) == 0)
    def _():
        tail_ref[...] = jnp.zeros_like(tail_ref)

    x = x_ref[...]
    tb = x.shape[0]
    ext = jnp.concatenate([tail_ref[...], x], axis=0)
    acc = b_ref[...] + w_ref[SSM_CONV - 1:SSM_CONV, :] * x
    for k in range(1, SSM_CONV):
        shifted = pltpu.roll(ext, k, 0)[SUBLANES:, :]
        acc = acc + w_ref[SSM_CONV - 1 - k:SSM_CONV - k, :] * shifted
    o_ref[...] = _silu(acc)
    tail_ref[...] = x[tb - SUBLANES:, :]


def _conv(zx, w, b, *, col0, tb, tc, name):
    s = zx.shape[0]
    cb0 = col0 // tc
    return pl.pallas_call(
        _conv_kernel,
        grid=(SSM_CONV_DIM // tc, s // tb),
        in_specs=[pl.BlockSpec((tb, tc), lambda c, r: (r, c + cb0)),
                  pl.BlockSpec((SSM_CONV, tc), lambda c, r: (0, c)),
                  pl.BlockSpec((1, tc), lambda c, r: (0, c))],
        out_specs=pl.BlockSpec((tb, tc), lambda c, r: (r, c)),
        out_shape=jax.ShapeDtypeStruct((s, SSM_CONV_DIM), F32),
        scratch_shapes=[pltpu.VMEM((SUBLANES, tc), F32)],
        compiler_params=_cparams(2),
        name=name,
    )(zx, w, b.reshape(1, SSM_CONV_DIM))


def _ssd_lane(l):
    return l // 2 + ODD_SLOT * (l % 2)


def _ssd_consts():
    c = CHUNK
    ltri = (np.arange(c)[None, :] <= np.arange(c)[:, None]).astype(np.float32)
    expand = np.zeros((LANES, SSM_GW), np.float32)
    for l in range(SSM_HPG):
        expand[_ssd_lane(l), l * SSM_P:(l + 1) * SSM_P] = 1.0
    expand2 = np.concatenate([expand, expand], axis=0)
    return ltri, expand2


def _ssd_kernel(xs_ref, z_ref, b_ref, c_ref, dt_ref, bias_ref, alog_ref, dsk_ref, nw_ref,
                ltri_ref, ex2_ref, o_ref, st_ref, *, n_chunks):
    @pl.when(pl.program_id(1) == 0)
    def _():
        st_ref[...] = jnp.zeros_like(st_ref)

    tb = n_chunks * CHUNK
    dt = _softplus(dt_ref[...] + bias_ref[...])
    da = dt * (-jnp.exp(alog_ref[...]))
    cs = _dot(ltri_ref[...], jnp.concatenate(
        [_split_bf16(da[c * CHUNK:(c + 1) * CHUNK]) for c in range(n_chunks)], axis=1))
    cums = [cs[:, 2 * c * LANES:(2 * c + 1) * LANES] + cs[:, (2 * c + 1) * LANES:(2 * c + 2) * LANES]
            for c in range(n_chunks)]
    stack = jnp.concatenate([dt] + cums, axis=0)
    wide = _dot(_split_bf16(stack), ex2_ref[...])
    dtx, cumx = wide[:tb], wide[tb:]

    xs = xs_ref[...]
    xdt = xs * dtx
    lane = lax.broadcasted_iota(jnp.int32, (CHUNK, LANES), 1)
    lo = lane < CHUNK
    tril = (lax.broadcasted_iota(jnp.int32, (CHUNK, CHUNK), 1)
            <= lax.broadcasted_iota(jnp.int32, (CHUNK, CHUNK), 0))
    st = st_ref[...]
    ys = []
    for c in range(n_chunks):
        rows = slice(c * CHUNK, (c + 1) * CHUNK)
        bm = b_ref[rows, :].astype(BF16)
        cm = c_ref[rows, :].astype(BF16)
        cb = jnp.where(tril, _dot_nt(cm, bm), 0.0)
        cb2 = jnp.concatenate([cb, cb], axis=1)
        cum_t = cums[c].T
        rowsrc = jnp.concatenate([cum_t[0:SUBLANES], cum_t[ODD_SLOT:ODD_SLOT + SUBLANES]], axis=1)
        cumx_c = cumx[rows]
        cum_end = cumx_c[CHUNK - 1:CHUNK, :]
        xdt_c = xdt[rows]
        pieces = []
        for j in range(SSM_HPG // 2):
            colb = cumx_c[:, j * LANES:(j + 1) * LANES]
            rowb = jnp.broadcast_to(rowsrc[j:j + 1, :], (CHUNK, LANES))
            m = cb2 * jnp.exp(jnp.minimum(colb - rowb, 0.0))
            xp = xdt_c[:, j * LANES:(j + 1) * LANES]
            rhs = jnp.concatenate([jnp.where(lo, xp, 0.0), jnp.where(lo, 0.0, xp)], axis=0)
            pieces.append(_dot(m.astype(BF16), rhs.astype(BF16)))
        y = jnp.concatenate(pieces, axis=1)
        y = y + _dot(cm, st.astype(BF16)) * jnp.exp(cumx_c)
        st = (st * jnp.exp(cum_end)
              + _dot_tn(bm, (xdt_c * jnp.exp(cum_end - cumx_c)).astype(BF16)))
        ys.append(y)
    st_ref[...] = st
    y = jnp.concatenate(ys, axis=0) + xs * dsk_ref[...]
    y = y * _silu(z_ref[...])
    ms = jnp.mean(y * y, axis=-1, keepdims=True)
    o_ref[...] = (y * lax.rsqrt(ms + EPS) * nw_ref[...]).astype(o_ref.dtype)


def _ssd(conv, proj, z_col0, dt_arr, dt_col0, dt_bias, a_log, d_skip, norm_w, *, n_chunks,
         name):
    s = conv.shape[0]
    tb = n_chunks * CHUNK
    ltri, expand2 = _ssd_consts()
    gw = SSM_GW // LANES
    const2 = lambda a: pl.BlockSpec(a.shape, lambda g, r: (0, 0))
    return pl.pallas_call(
        functools.partial(_ssd_kernel, n_chunks=n_chunks),
        grid=(SSM_GROUPS, s // tb),
        in_specs=[pl.BlockSpec((tb, SSM_GW), lambda g, r: (r, g)),
                  pl.BlockSpec((tb, SSM_GW), lambda g, r: (r, z_col0 // SSM_GW + g)),
                  pl.BlockSpec((tb, SSM_N), lambda g, r: (r, SSM_GROUPS * gw + g)),
                  pl.BlockSpec((tb, SSM_N), lambda g, r: (r, SSM_GROUPS * (gw + 1) + g)),
                  pl.BlockSpec((tb, LANES), lambda g, r: (r, dt_col0 // LANES + g)),
                  pl.BlockSpec((1, LANES), lambda g, r: (0, g)),
                  pl.BlockSpec((1, LANES), lambda g, r: (0, g)),
                  pl.BlockSpec((1, SSM_GW), lambda g, r: (0, g)),
                  pl.BlockSpec((1, SSM_GW), lambda g, r: (0, g)),
                  const2(ltri), const2(expand2)],
        out_specs=pl.BlockSpec((tb, SSM_GW), lambda g, r: (r, g)),
        out_shape=jax.ShapeDtypeStruct((s, SSM_INNER), BF16),
        scratch_shapes=[pltpu.VMEM((SSM_N, SSM_GW), F32)],
        compiler_params=_cparams(2),
        name=name,
    )(conv, proj, conv, conv, dt_arr, dt_bias, a_log, d_skip, norm_w.reshape(1, SSM_INNER),
      jnp.asarray(ltri, BF16), jnp.asarray(expand2, BF16))


IN_SIZES = (MLA_RANK, MLA_RANK, MLA_ROPE, HG_WIDTH, HG_WIDTH, HG_WIDTH, HG_WIDTH,
            SSM_INNER, SSM_CONV_DIM, SSM_HEADS, 3 * D_MODEL)
IN_OFFS = tuple(int(v) for v in np.cumsum((0,) + IN_SIZES))
PROJ_HG = 0
PROJ_Z = PROJ_HG + 4 * HG_WIDTH
PROJ_XBC = PROJ_Z + SSM_INNER
PROJ_MID_N = PROJ_XBC + SSM_CONV_DIM
SMALL_DT = 2 * MLA_RANK
SMALL_N = SMALL_DT + SSM_GROUPS * LANES


def _dt_layout():
    idx = np.zeros((SSM_GROUPS * LANES,), np.int32)
    valid = np.zeros((SSM_GROUPS * LANES,), np.float32)
    for g in range(SSM_GROUPS):
        for l in range(SSM_HPG):
            idx[g * LANES + _ssd_lane(l)] = g * SSM_HPG + l
            valid[g * LANES + _ssd_lane(l)] = 1.0
    return idx, valid


def _rope_tables(s):
    inv = 1.0 / (ROPE_THETA ** (jnp.arange(0, MLA_ROPE, 2, dtype=F32) / MLA_ROPE))
    ang = jnp.arange(s, dtype=F32)[:, None] * inv[None, :]
    cos, sin = jnp.cos(ang), jnp.sin(ang)
    z32 = jnp.zeros_like(cos)
    z64 = jnp.concatenate([z32, z32], axis=1)
    ta = jnp.concatenate([cos, cos, z64], axis=1)
    tb = jnp.concatenate([-sin, z32, z64], axis=1)
    tc = jnp.concatenate([z32, sin, z64], axis=1)
    return ta, tb, tc


def kernel(x, ffn1_norm, ffn1_wi, ffn1_wo, mix_norm, w_in, mla_q_norm, mla_w_uq, mla_kv_norm,
           mla_w_ukv, hgrn_lb_logits, hgrn_norm, ssm_conv_w, ssm_conv_b, ssm_a_log, ssm_dt_bias,
           ssm_d, ssm_norm, w_o_mla, w_o_hgrn, w_o_ssm, w_out, ffn2_norm, ffn2_wi, ffn2_wo,
           final_norm):
    assert x.shape[0] == 1 and x.shape[2] == D_MODEL
    s = x.shape[1]
    x = x[0]
    rope = _rope_tables(s)
    dt_idx, dt_valid = _dt_layout()
    o = IN_OFFS
    qk_scale = (MLA_NOPE + MLA_ROPE) ** -0.5 * float(np.log2(np.e))

    def ffn(x, norm_w, wi, wo, l, tag):
        h = _norm_mm(x, norm_w, [(wi, l, 0), (wi, l, D_FF)], n=D_FF, epi=_epi_swiglu,
                     out_dtype=BF16, tm=1024, tn=512, name=tag + "_up")
        return _mm_res(h, wo, l, x, scale=0.5, tm=512, tn=512, name=tag + "_down")

    for l in range(DEPTH):
        x = ffn(x, ffn1_norm[l], ffn1_wi, ffn1_wo, l, f"l{l}_ffn1")

        nw = mix_norm[l]
        w_mid = w_in[l, :, o[3]:o[9]].astype(BF16)
        proj = _norm_mm(x, nw, [(w_mid, None, 0)], n=PROJ_MID_N, epi=_epi_id, out_dtype=F32,
                        tm=1024, tn=512, name=f"l{l}_in_main")
        w_dt = w_in[l, :, o[9]:o[10]][:, dt_idx] * dt_valid
        w_small = jnp.concatenate([w_in[l, :, o[0]:o[2]], w_dt], axis=1).astype(BF16)
        lat_dt = _norm_mm(x, nw, [(w_small, None, 0)], n=SMALL_N, epi=_epi_id, out_dtype=F32,
                          tm=1024, tn=512, name=f"l{l}_in_small")
        w_kpe = jnp.pad(w_in[l, :, o[2]:o[3]], ((0, 0), (0, LANES - MLA_ROPE))).astype(BF16)
        k_rot = _norm_mm(x, nw, [(w_kpe, None, 0)], n=LANES, epi=_epi_rope_k, out_dtype=BF16,
                         tm=1024, tn=LANES, extras=rope, name=f"l{l}_in_kpe")
        w_gates = w_in[l, :, o[10]:o[11]].astype(BF16)
        gates = _norm_mm(x, nw, [(w_gates, None, 0)], n=3 * D_MODEL, epi=_epi_sigmoid,
                         out_dtype=BF16, tm=1024, tn=512, name=f"l{l}_in_gates")

        wq = mla_w_uq[l].reshape(MLA_RANK, MLA_HEADS, MLA_NOPE + MLA_ROPE)
        wq = jnp.pad(wq, ((0, 0), (0, 0), (0, MLA_QK_PAD - MLA_NOPE - MLA_ROPE)))
        wq = wq.reshape(MLA_RANK, MLA_HEADS * MLA_QK_PAD).astype(BF16)
        q = _norm_mm(lat_dt, mla_q_norm[l], [(wq, None, 0)], n=MLA_HEADS * MLA_QK_PAD,
                     epi=functools.partial(_epi_rope_q, scale=qk_scale), out_dtype=BF16,
                     tm=1024, tn=512, xcol=0, extras=rope, name=f"l{l}_mla_q")
        wkv = mla_w_ukv[l].reshape(MLA_RANK, MLA_HEADS, MLA_NOPE + MLA_V)
        wkv = jnp.concatenate([wkv[:, :, :MLA_NOPE].reshape(MLA_RANK, -1),
                               wkv[:, :, MLA_NOPE:].reshape(MLA_RANK, -1)], axis=1).astype(BF16)
        kv = _norm_mm(lat_dt, mla_kv_norm[l], [(wkv, None, 0)], n=wkv.shape[1], epi=_epi_id,
                      out_dtype=BF16, tm=1024, tn=512, xcol=1, name=f"l{l}_mla_kv")
        y_a = _flash(q, kv, k_rot, t=FLASH_T, name=f"l{l}_mla_attn")

        y_b = _hgrn(proj, hgrn_lb_logits, hgrn_norm[l], col0=PROJ_HG, layer=l, n_chunks=4,
                    n_heads=4, name=f"l{l}_hgrn")

        conv = _conv(proj, ssm_conv_w[l], ssm_conv_b[l], col0=PROJ_XBC, tb=256, tc=1024,
                     name=f"l{l}_ssm_conv")
        y_c = _ssd(conv, proj, PROJ_Z, lat_dt, SMALL_DT,
                   (ssm_dt_bias[l][dt_idx] * dt_valid).reshape(1, -1),
                   (ssm_a_log[l][dt_idx] * dt_valid).reshape(1, -1),
                   jnp.repeat(ssm_d[l], SSM_P).reshape(1, -1), ssm_norm[l],
                   n_chunks=4, name=f"l{l}_ssm_scan")

        m = _merge(y_a, y_b, y_c, w_o_mla, w_o_hgrn, w_o_ssm, l, gates, tm=512, tn=256,
                   name=f"l{l}_merge")
        x = _mm_res(m, w_out, l, x, scale=1.0, tm=1024, tn=512, name=f"l{l}_out")

        x = ffn(x, ffn2_norm[l], ffn2_wi, ffn2_wo, l, f"l{l}_ffn2")

    return _rmsnorm(x, final_norm, tm=512, name="final_norm")[None]
```

```python
import functools

import numpy as np
import jax
import jax.numpy as jnp
from jax import lax
from jax.experimental import pallas as pl
from jax.experimental.pallas import tpu as pltpu

F32 = jnp.float32
BF16 = jnp.bfloat16

D_MODEL = 2048
DEPTH = 2
CHUNK = 64
EPS = 1e-6

MLA_HEADS = 16
MLA_RANK = 512
MLA_NOPE = 128
MLA_ROPE = 64
MLA_V = 128
MLA_QK_PAD = 256
ROPE_THETA = 10000.0
FLASH_T = 512
FLASH_HEADS = 1

HG_HEADS = 16
HG_D = 128
HG_WIDTH = HG_HEADS * HG_D
HG_LEVELS = 7

SSM_INNER = 4096
SSM_P = 64
SSM_HEADS = 64
SSM_GROUPS = 8
SSM_HPG = SSM_HEADS // SSM_GROUPS
SSM_N = 128
SSM_GW = SSM_INNER // SSM_GROUPS
SSM_CONV = 4
SSM_CONV_DIM = SSM_INNER + 2 * SSM_GROUPS * SSM_N
ODD_SLOT = 8

D_FF = 5632

LANES = 128
SUBLANES = 8
VMEM_LIMIT = 52 * 1024 * 1024


def _cparams(n_axes):
    return pltpu.CompilerParams(
        dimension_semantics=("arbitrary",) * n_axes, vmem_limit_bytes=VMEM_LIMIT)


def _sigmoid(x):
    return jax.nn.sigmoid(x)


def _silu(x):
    return x * jax.nn.sigmoid(x)


def _softplus(x):
    return jnp.maximum(x, 0.0) + jnp.log1p(jnp.exp(-jnp.abs(x)))


def _split_bf16(x):
    hi = x.astype(BF16)
    mid = (x - hi.astype(F32)).astype(BF16)
    return jnp.concatenate([hi, mid], axis=1)


def _dot(a, b):
    return jnp.dot(a, b, preferred_element_type=F32)


def _dot_nt(a, b):
    return lax.dot_general(a, b, (((1,), (1,)), ((), ())), preferred_element_type=F32)


def _dot_tn(a, b):
    return lax.dot_general(a, b, (((0,), (0,)), ((), ())), preferred_element_type=F32)


def _norm_mm_kernel(*refs, n_w, n_ex, epi):
    x_ref, nw_ref = refs[0], refs[1]
    w_refs = refs[2:2 + n_w]
    ex_refs = refs[2 + n_w:2 + n_w + n_ex]
    o_ref = refs[2 + n_w + n_ex]
    xn_ref = refs[3 + n_w + n_ex]

    @pl.when(pl.program_id(1) == 0)
    def _():
        x = x_ref[...]
        ms = jnp.mean(x * x, axis=-1, keepdims=True)
        xn_ref[...] = (x * lax.rsqrt(ms + EPS) * nw_ref[...]).astype(BF16)

    xn = xn_ref[...]
    accs = [_dot(xn, w[...].astype(BF16)) for w in w_refs]
    o_ref[...] = epi(accs, [e[...] for e in ex_refs]).astype(o_ref.dtype)


def _w_spec(w, k, tn, layer, col0):
    off = col0 // tn
    if w.ndim == 3:
        return pl.BlockSpec((None, k, tn), lambda i, j: (layer, 0, j + off))
    return pl.BlockSpec((k, tn), lambda i, j: (0, j + off))


def _norm_mm(x, nw, ws, *, n, epi, out_dtype, tm, tn, name, xcol=0, extras=()):
    s = x.shape[0]
    k = nw.shape[-1]
    grid = (s // tm, n // tn)
    in_specs = [pl.BlockSpec((tm, k), lambda i, j: (i, xcol)),
                pl.BlockSpec((1, k), lambda i, j: (0, 0))]
    for w, layer, col0 in ws:
        in_specs.append(_w_spec(w, k, tn, layer, col0))
    for e in extras:
        in_specs.append(pl.BlockSpec((tm, e.shape[1]), lambda i, j: (i, 0)))
    return pl.pallas_call(
        functools.partial(_norm_mm_kernel, n_w=len(ws), n_ex=len(extras), epi=epi),
        grid=grid,
        in_specs=in_specs,
        out_specs=pl.BlockSpec((tm, tn), lambda i, j: (i, j)),
        out_shape=jax.ShapeDtypeStruct((s, n), out_dtype),
        scratch_shapes=[pltpu.VMEM((tm, k), BF16)],
        compiler_params=_cparams(2),
        name=name,
    )(x, nw.reshape(1, k), *[w for w, _, _ in ws], *extras)


def _epi_id(accs, ex):
    return accs[0]


def _epi_sigmoid(accs, ex):
    return _sigmoid(accs[0])


def _epi_swiglu(accs, ex):
    return _silu(accs[0]) * accs[1]


def _rope_chunk(c, ta, tb, tc):
    return c * ta + pltpu.roll(c, 96, 1) * tb + pltpu.roll(c, 32, 1) * tc


def _epi_rope_k(accs, ex):
    return _rope_chunk(accs[0], *ex)


def _epi_rope_q(accs, ex, *, scale):
    acc = accs[0]
    outs = []
    for h in range(acc.shape[1] // MLA_QK_PAD):
        base = h * MLA_QK_PAD
        outs.append(acc[:, base:base + LANES] * scale)
        outs.append(_rope_chunk(acc[:, base + LANES:base + 2 * LANES], *ex) * scale)
    return jnp.concatenate(outs, axis=1)


def _mm_res_kernel(a_ref, w_ref, r_ref, o_ref, *, scale):
    o_ref[...] = r_ref[...] + scale * _dot(a_ref[...], w_ref[...].astype(BF16))


def _mm_res(a, w, layer, res, *, scale, tm, tn, name):
    s, k = a.shape
    n = w.shape[-1]
    return pl.pallas_call(
        functools.partial(_mm_res_kernel, scale=scale),
        grid=(s // tm, n // tn),
        in_specs=[pl.BlockSpec((tm, k), lambda i, j: (i, 0)),
                  _w_spec(w, k, tn, layer, 0),
                  pl.BlockSpec((tm, tn), lambda i, j: (i, j))],
        out_specs=pl.BlockSpec((tm, tn), lambda i, j: (i, j)),
        out_shape=jax.ShapeDtypeStruct((s, n), F32),
        compiler_params=_cparams(2),
        name=name,
    )(a, w, res)


def _merge_kernel(a_ref, b_ref, c_ref, wa_ref, wb_ref, wc_ref, ga_ref, gb_ref, gc_ref, o_ref):
    ya = _dot(a_ref[...], wa_ref[...].astype(BF16))
    yb = _dot(b_ref[...], wb_ref[...].astype(BF16))
    yc = _dot(c_ref[...], wc_ref[...].astype(BF16))
    o = (ga_ref[...].astype(F32) * ya + gb_ref[...].astype(F32) * yb
         + gc_ref[...].astype(F32) * yc)
    o_ref[...] = o.astype(o_ref.dtype)


def _merge(a, b, c, wa, wb, wc, layer, gates, *, tm, tn, name):
    s = a.shape[0]
    n = wa.shape[-1]
    nj = n // tn
    row = lambda arr: pl.BlockSpec((tm, arr.shape[1]), lambda i, j: (i, 0))
    col = lambda arr: _w_spec(arr, arr.shape[-2], tn, layer, 0)
    gate = lambda b_: pl.BlockSpec((tm, tn), lambda i, j, b_=b_: (i, j + b_ * nj))
    return pl.pallas_call(
        _merge_kernel,
        grid=(s // tm, nj),
        in_specs=[row(a), row(b), row(c), col(wa), col(wb), col(wc),
                  gate(0), gate(1), gate(2)],
        out_specs=pl.BlockSpec((tm, tn), lambda i, j: (i, j)),
        out_shape=jax.ShapeDtypeStruct((s, n), BF16),
        compiler_params=_cparams(2),
        name=name,
    )(a, b, c, wa, wb, wc, gates, gates, gates)


def _rms_kernel(x_ref, w_ref, o_ref):
    x = x_ref[...]
    ms = jnp.mean(x * x, axis=-1, keepdims=True)
    o_ref[...] = x * lax.rsqrt(ms + EPS) * w_ref[...]


def _rmsnorm(x, w, *, tm, name):
    s, d = x.shape
    return pl.pallas_call(
        _rms_kernel,
        grid=(s // tm,),
        in_specs=[pl.BlockSpec((tm, d), lambda i: (i, 0)),
                  pl.BlockSpec((1, d), lambda i: (0, 0))],
        out_specs=pl.BlockSpec((tm, d), lambda i: (i, 0)),
        out_shape=jax.ShapeDtypeStruct((s, d), F32),
        compiler_params=_cparams(1),
        name=name,
    )(x, w.reshape(1, d))


def _flash_kernel(q_ref, kn_ref, kr_ref, v_ref, o_ref, *, t, nh):
    qb = pl.program_id(1)

    def scores(e, kb):
        off = pl.multiple_of(kb * t, t)
        k = jnp.concatenate([kn_ref[pl.ds(off, t), e * MLA_NOPE:(e + 1) * MLA_NOPE],
                             kr_ref[pl.ds(off, t), :]], axis=1)
        return _dot_nt(q_ref[:, e * MLA_QK_PAD:(e + 1) * MLA_QK_PAD], k)

    def update(e, m, l, acc, s, kb):
        v = v_ref[pl.ds(pl.multiple_of(kb * t, t), t), e * MLA_V:(e + 1) * MLA_V]
        m_new = jnp.maximum(m, jnp.max(s, axis=-1, keepdims=True))
        alpha = jnp.exp2(m - m_new)
        p = jnp.exp2(s - m_new)
        l = alpha * l + jnp.sum(p, axis=-1, keepdims=True)
        acc = alpha * acc + _dot(p.astype(BF16), v)
        return m_new, l, acc

    def two_blocks(carries, kb0, kb1, mask):
        ss = [(scores(e, kb0), scores(e, kb1)) for e in range(nh)]
        out = []
        for e in range(nh):
            s0, s1 = ss[e]
            if mask is not None:
                s1 = jnp.where(mask, s1, -jnp.inf)
            out.append(update(e, *update(e, *carries[e], s0, kb0), s1, kb1))
        return tuple(out)

    def multi(j, carries):
        return two_blocks(carries, 2 * j, 2 * j + 1, None)

    init = tuple((jnp.full((t, 1), -jnp.inf, F32), jnp.zeros((t, 1), F32),
                  jnp.zeros((t, MLA_V), F32)) for _ in range(nh))
    carries = lax.fori_loop(0, qb // 2, multi, init)

    def diag_mask():
        row = lax.broadcasted_iota(jnp.int32, (t, t), 0) // CHUNK
        col = lax.broadcasted_iota(jnp.int32, (t, t), 1) // CHUNK
        return col <= row

    def tail_pair(carries):
        return two_blocks(carries, qb - 1, qb, diag_mask())

    def tail_single(carries):
        mask = diag_mask()
        return tuple(update(e, *carries[e], jnp.where(mask, scores(e, qb), -jnp.inf), qb)
                     for e in range(nh))

    carries = lax.cond(qb % 2 == 1, tail_pair, tail_single, carries)
    o_ref[...] = jnp.concatenate([acc / l for _, l, acc in carries], axis=1).astype(o_ref.dtype)


def _flash(q, kv, kr, *, t, nh, name):
    s = q.shape[0]
    v_blk0 = MLA_HEADS // nh
    return pl.pallas_call(
        functools.partial(_flash_kernel, t=t, nh=nh),
        grid=(MLA_HEADS // nh, s // t),
        in_specs=[pl.BlockSpec((t, nh * MLA_QK_PAD), lambda h, i: (i, h)),
                  pl.BlockSpec((s, nh * MLA_NOPE), lambda h, i: (0, h)),
                  pl.BlockSpec((s, LANES), lambda h, i: (0, 0)),
                  pl.BlockSpec((s, nh * MLA_V), lambda h, i: (0, v_blk0 + h))],
        out_specs=pl.BlockSpec((t, nh * MLA_V), lambda h, i: (i, h)),
        out_shape=jax.ShapeDtypeStruct((s, MLA_HEADS * MLA_V), BF16),
        compiler_params=_cparams(2),
        name=name,
    )(q, kv, kr, kv)


def _hgrn_consts():
    c = CHUNK
    t = np.arange(c)[:, None]
    u = np.arange(c)[None, :]
    ltri = (u <= t).astype(np.float32)
    mask = np.zeros((HG_LEVELS, c, c), np.float32)
    mask[0] = (u == t)
    for lvl in range(1, HG_LEVELS):
        m = c >> lvl
        mask[lvl] = ((t // (2 * m)) == (u // (2 * m))) & (t % (2 * m) >= m) & (u % (2 * m) < m)
    return ltri, mask


def _row_masks():
    row = lax.broadcasted_iota(jnp.int32, (CHUNK, HG_D), 0)
    return row % 4 == 0, row % 4 == 1, row % 4 == 2, row % 2 == 0


def _level_refs(b, row_masks):
    c = CHUNK
    q0, q1, q2, even = row_masks
    refs = []
    for m in (32, 16, 8):
        pieces = [jnp.broadcast_to(b[u0:u0 + 1, :], (2 * m, HG_D)) for u0 in range(m, c, 2 * m)]
        refs.append(jnp.concatenate(pieces, axis=0) if len(pieces) > 1 else pieces[0])
    b3 = b.reshape(c // SUBLANES, SUBLANES, HG_D)
    refs.append(jnp.broadcast_to(b3[:, 4:5, :], b3.shape).reshape(c, HG_D))
    up1 = pltpu.roll(b, c - 1, 0)
    up2 = pltpu.roll(b, c - 2, 0)
    dn1 = pltpu.roll(b, 1, 0)
    refs.append(jnp.where(q0, up2, jnp.where(q1, up1, jnp.where(q2, b, dn1))))
    refs.append(jnp.where(even, up1, b))
    return refs


def _neg_abs(x):
    bits = lax.bitcast_convert_type(x, jnp.uint32) | jnp.uint32(0x80000000)
    return lax.bitcast_convert_type(bits, F32)


def _hgrn_kernel(q_ref, f_ref, i_ref, g_ref, lg_ref, nw_ref, ltri_ref, mask_ref, o_ref, st_ref,
                 *, layer, n_chunks, n_heads):
    @pl.when(pl.program_id(1) == 0)
    def _():
        st_ref[...] = jnp.zeros_like(st_ref)

    lg = lg_ref[...]
    e = jnp.exp(lg - jnp.max(lg, axis=0, keepdims=True))
    p = e / jnp.sum(e, axis=0, keepdims=True)
    lb = jnp.sum(p[0:layer + 1], axis=0, keepdims=True) - p[0:1]

    ltri = ltri_ref[...]
    row_masks = _row_masks()
    f_all = lb + (1.0 - lb) * _sigmoid(f_ref[...])
    logf_all = jnp.log(f_all)
    k_all = 1.0 - f_all
    q_all = _silu(q_ref[...]) * HG_D ** -0.5
    v_all = i_ref[...].astype(BF16)
    gate_all = nw_ref[...] * _silu(g_ref[...])
    outs = [[None] * n_heads for _ in range(n_chunks)]
    for h in range(n_heads):
        cols = slice(h * HG_D, (h + 1) * HG_D)
        logf = jnp.concatenate(
            [_split_bf16(logf_all[c * CHUNK:(c + 1) * CHUNK, cols]) for c in range(n_chunks)],
            axis=1)
        b_all = _dot(ltri, logf)
        st = st_ref[h]
        for c in range(n_chunks):
            rows = slice(c * CHUNK, (c + 1) * CHUNK)
            q, k, v = q_all[rows, cols], k_all[rows, cols], v_all[rows, cols]
            b = (b_all[:, 2 * c * HG_D:(2 * c + 1) * HG_D]
                 + b_all[:, (2 * c + 1) * HG_D:(2 * c + 2) * HG_D])
            ex = [None] + [_neg_abs(b - r) for r in _level_refs(b, row_masks)]
            b_last = b[CHUNK - 1:CHUNK, :]
            q16, k16 = q.astype(BF16), k.astype(BF16)
            att = _dot_nt(q16, k16) * mask_ref[0]
            for lvl in range(1, HG_LEVELS):
                w = jnp.exp(ex[lvl]).astype(BF16)
                att = att + _dot_nt(q16 * w, k16 * w) * mask_ref[lvl]
            o = _dot_nt((q * jnp.exp(b)).astype(BF16), st.astype(BF16))
            o = o + _dot(att.astype(BF16), v)
            kd = (k * jnp.exp(b_last - b)).astype(BF16)
            st = st * jnp.exp(b_last) + _dot_tn(v, kd)
            ms = jnp.mean(o * o, axis=-1, keepdims=True)
            outs[c][h] = o * lax.rsqrt(ms + EPS) * gate_all[rows, cols]
        st_ref[h] = st
    o_ref[...] = jnp.concatenate(
        [jnp.concatenate(r, axis=1) for r in outs], axis=0).astype(o_ref.dtype)


def _hgrn(hg, lb_logits, norm_w, *, col0, layer, n_chunks, n_heads, name):
    s = hg.shape[0]
    tb = n_chunks * CHUNK
    tw = n_heads * HG_D
    nhb = HG_HEADS // n_heads
    ltri, mask = _hgrn_consts()
    nl = lb_logits.shape[0]
    cb0 = col0 // tw
    piece = lambda b_: pl.BlockSpec((tb, tw), lambda h, c, b_=b_: (c, cb0 + h + b_ * nhb))
    return pl.pallas_call(
        functools.partial(_hgrn_kernel, layer=layer, n_chunks=n_chunks, n_heads=n_heads),
        grid=(nhb, s // tb),
        in_specs=[piece(0), piece(1), piece(2), piece(3),
                  pl.BlockSpec((nl, tw), lambda h, c: (0, h)),
                  pl.BlockSpec((1, tw), lambda h, c: (0, h)),
                  pl.BlockSpec(ltri.shape, lambda h, c: (0, 0)),
                  pl.BlockSpec(mask.shape, lambda h, c: (0, 0, 0))],
        out_specs=pl.BlockSpec((tb, tw), lambda h, c: (c, h)),
        out_shape=jax.ShapeDtypeStruct((s, HG_WIDTH), BF16),
        scratch_shapes=[pltpu.VMEM((n_heads, HG_D, HG_D), F32)],
        compiler_params=_cparams(2),
        name=name,
    )(hg, hg, hg, hg, lb_logits, norm_w.reshape(1, HG_WIDTH),
      jnp.asarray(ltri, BF16), jnp.asarray(mask, F32))


def _conv_kernel(x_ref, w_ref, b_ref, o_ref, tail_ref):
    @pl.when(pl.program_id(1) == 0)
    def _():
        tail_ref[...] = jnp.zeros_like(tail_ref)

    x = x_ref[...]
    tb = x.shape[0]
    ext = jnp.concatenate([tail_ref[...], x], axis=0)
    acc = b_ref[...] + w_ref[SSM_CONV - 1:SSM_CONV, :] * x
    for k in range(1, SSM_CONV):
        shifted = pltpu.roll(ext, k, 0)[SUBLANES:, :]
        acc = acc + w_ref[SSM_CONV - 1 - k:SSM_CONV - k, :] * shifted
    o_ref[...] = _silu(acc)
    tail_ref[...] = x[tb - SUBLANES:, :]


def _conv(zx, w, b, *, col0, tb, tc, name):
    s = zx.shape[0]
    cb0 = col0 // tc
    return pl.pallas_call(
        _conv_kernel,
        grid=(SSM_CONV_DIM // tc, s // tb),
        in_specs=[pl.BlockSpec((tb, tc), lambda c, r: (r, c + cb0)),
                  pl.BlockSpec((SSM_CONV, tc), lambda c, r: (0, c)),
                  pl.BlockSpec((1, tc), lambda c, r: (0, c))],
        out_specs=pl.BlockSpec((tb, tc), lambda c, r: (r, c)),
        out_shape=jax.ShapeDtypeStruct((s, SSM_CONV_DIM), F32),
        scratch_shapes=[pltpu.VMEM((SUBLANES, tc), F32)],
        compiler_params=_cparams(2),
        name=name,
    )(zx, w, b.reshape(1, SSM_CONV_DIM))


def _ssd_lane(l):
    return l // 2 + ODD_SLOT * (l % 2)


def _ssd_consts():
    c = CHUNK
    ltri = (np.arange(c)[None, :] <= np.arange(c)[:, None]).astype(np.float32)
    expand = np.zeros((LANES, SSM_GW), np.float32)
    for l in range(SSM_HPG):
        expand[_ssd_lane(l), l * SSM_P:(l + 1) * SSM_P] = 1.0
    expand2 = np.concatenate([expand, expand], axis=0)
    return ltri, expand2


def _ssd_kernel(xs_ref, z_ref, b_ref, c_ref, dt_ref, bias_ref, alog_ref, dsk_ref, nw_ref,
                ltri_ref, ex2_ref, o_ref, st_ref, *, n_chunks):
    @pl.when(pl.program_id(1) == 0)
    def _():
        st_ref[...] = jnp.zeros_like(st_ref)

    tb = n_chunks * CHUNK
    dt = _softplus(dt_ref[...] + bias_ref[...])
    da = dt * (-jnp.exp(alog_ref[...]))
    cs = _dot(ltri_ref[...], jnp.concatenate(
        [_split_bf16(da[c * CHUNK:(c + 1) * CHUNK]) for c in range(n_chunks)], axis=1))
    cums = [cs[:, 2 * c * LANES:(2 * c + 1) * LANES] + cs[:, (2 * c + 1) * LANES:(2 * c + 2) * LANES]
            for c in range(n_chunks)]
    stack = jnp.concatenate([dt] + cums, axis=0)
    wide = _dot(_split_bf16(stack), ex2_ref[...])
    dtx, cumx = wide[:tb], wide[tb:]

    xs = xs_ref[...]
    xdt = xs * dtx
    lane = lax.broadcasted_iota(jnp.int32, (CHUNK, LANES), 1)
    lo = lane < CHUNK
    tril = (lax.broadcasted_iota(jnp.int32, (CHUNK, CHUNK), 1)
            <= lax.broadcasted_iota(jnp.int32, (CHUNK, CHUNK), 0))
    st = st_ref[...]
    ys = []
    for c in range(n_chunks):
        rows = slice(c * CHUNK, (c + 1) * CHUNK)
        bm = b_ref[rows, :].astype(BF16)
        cm = c_ref[rows, :].astype(BF16)
        cb = jnp.where(tril, _dot_nt(cm, bm), 0.0)
        cb2 = jnp.concatenate([cb, cb], axis=1)
        cum_t = cums[c].T
        rowsrc = jnp.concatenate([cum_t[0:SUBLANES], cum_t[ODD_SLOT:ODD_SLOT + SUBLANES]], axis=1)
        cumx_c = cumx[rows]
        cum_end = cumx_c[CHUNK - 1:CHUNK, :]
        xdt_c = xdt[rows]
        pieces = []
        for j in range(SSM_HPG // 2):
            colb = cumx_c[:, j * LANES:(j + 1) * LANES]
            rowb = jnp.broadcast_to(rowsrc[j:j + 1, :], (CHUNK, LANES))
            m = cb2 * jnp.exp(jnp.minimum(colb - rowb, 0.0))
            xp = xdt_c[:, j * LANES:(j + 1) * LANES]
            rhs = jnp.concatenate([jnp.where(lo, xp, 0.0), jnp.where(lo, 0.0, xp)], axis=0)
            pieces.append(_dot(m.astype(BF16), rhs.astype(BF16)))
        y = jnp.concatenate(pieces, axis=1)
        y = y + _dot(cm, st.astype(BF16)) * jnp.exp(cumx_c)
        st = (st * jnp.exp(cum_end)
              + _dot_tn(bm, (xdt_c * jnp.exp(cum_end - cumx_c)).astype(BF16)))
        ys.append(y)
    st_ref[...] = st
    y = jnp.concatenate(ys, axis=0) + xs * dsk_ref[...]
    y = y * _silu(z_ref[...])
    ms = jnp.mean(y * y, axis=-1, keepdims=True)
    o_ref[...] = (y * lax.rsqrt(ms + EPS) * nw_ref[...]).astype(o_ref.dtype)


def _ssd(conv, proj, z_col0, dt_arr, dt_col0, dt_bias, a_log, d_skip, norm_w, *, n_chunks,
         name):
    s = conv.shape[0]
    tb = n_chunks * CHUNK
    ltri, expand2 = _ssd_consts()
    gw = SSM_GW // LANES
    const2 = lambda a: pl.BlockSpec(a.shape, lambda g, r: (0, 0))
    return pl.pallas_call(
        functools.partial(_ssd_kernel, n_chunks=n_chunks),
        grid=(SSM_GROUPS, s // tb),
        in_specs=[pl.BlockSpec((tb, SSM_GW), lambda g, r: (r, g)),
                  pl.BlockSpec((tb, SSM_GW), lambda g, r: (r, z_col0 // SSM_GW + g)),
                  pl.BlockSpec((tb, SSM_N), lambda g, r: (r, SSM_GROUPS * gw + g)),
                  pl.BlockSpec((tb, SSM_N), lambda g, r: (r, SSM_GROUPS * (gw + 1) + g)),
                  pl.BlockSpec((tb, LANES), lambda g, r: (r, dt_col0 // LANES + g)),
                  pl.BlockSpec((1, LANES), lambda g, r: (0, g)),
                  pl.BlockSpec((1, LANES), lambda g, r: (0, g)),
                  pl.BlockSpec((1, SSM_GW), lambda g, r: (0, g)),
                  pl.BlockSpec((1, SSM_GW), lambda g, r: (0, g)),
                  const2(ltri), const2(expand2)],
        out_specs=pl.BlockSpec((tb, SSM_GW), lambda g, r: (r, g)),
        out_shape=jax.ShapeDtypeStruct((s, SSM_INNER), BF16),
        scratch_shapes=[pltpu.VMEM((SSM_N, SSM_GW), F32)],
        compiler_params=_cparams(2),
        name=name,
    )(conv, proj, conv, conv, dt_arr, dt_bias, a_log, d_skip, norm_w.reshape(1, SSM_INNER),
      jnp.asarray(ltri, BF16), jnp.asarray(expand2, BF16))


IN_SIZES = (MLA_RANK, MLA_RANK, MLA_ROPE, HG_WIDTH, HG_WIDTH, HG_WIDTH, HG_WIDTH,
            SSM_INNER, SSM_CONV_DIM, SSM_HEADS, 3 * D_MODEL)
IN_OFFS = tuple(int(v) for v in np.cumsum((0,) + IN_SIZES))
PROJ_HG = 0
PROJ_Z = PROJ_HG + 4 * HG_WIDTH
PROJ_XBC = PROJ_Z + SSM_INNER
PROJ_MID_N = PROJ_XBC + SSM_CONV_DIM
SMALL_DT = 2 * MLA_RANK
SMALL_N = SMALL_DT + SSM_GROUPS * LANES


def _dt_place(a):
    place = np.zeros((SSM_HEADS, SSM_GROUPS * LANES), np.float32)
    for g in range(SSM_GROUPS):
        for l in range(SSM_HPG):
            place[g * SSM_HPG + l, g * LANES + _ssd_lane(l)] = 1.0
    return jnp.dot(a, place, precision=lax.Precision.HIGHEST)


def _rope_tables(s):
    inv = 1.0 / (ROPE_THETA ** (jnp.arange(0, MLA_ROPE, 2, dtype=F32) / MLA_ROPE))
    ang = jnp.arange(s, dtype=F32)[:, None] * inv[None, :]
    cos, sin = jnp.cos(ang), jnp.sin(ang)
    z32 = jnp.zeros_like(cos)
    z64 = jnp.concatenate([z32, z32], axis=1)
    ta = jnp.concatenate([cos, cos, z64], axis=1)
    tb = jnp.concatenate([-sin, z32, z64], axis=1)
    tc = jnp.concatenate([z32, sin, z64], axis=1)
    return ta, tb, tc


def kernel(x, ffn1_norm, ffn1_wi, ffn1_wo, mix_norm, w_in, mla_q_norm, mla_w_uq, mla_kv_norm,
           mla_w_ukv, hgrn_lb_logits, hgrn_norm, ssm_conv_w, ssm_conv_b, ssm_a_log, ssm_dt_bias,
           ssm_d, ssm_norm, w_o_mla, w_o_hgrn, w_o_ssm, w_out, ffn2_norm, ffn2_wi, ffn2_wo,
           final_norm):
    assert x.shape[0] == 1 and x.shape[2] == D_MODEL
    s = x.shape[1]
    x = x[0]
    rope = _rope_tables(s)
    o = IN_OFFS
    qk_scale = (MLA_NOPE + MLA_ROPE) ** -0.5 * float(np.log2(np.e))
    ffn1_wo_b, ffn2_wo_b = ffn1_wo.astype(BF16), ffn2_wo.astype(BF16)
    w_o_b = [w.astype(BF16) for w in (w_o_mla, w_o_hgrn, w_o_ssm)]

    def ffn(x, norm_w, wi, wo_b, l, tag):
        h = _norm_mm(x, norm_w, [(wi, l, 0), (wi, l, D_FF)], n=D_FF, epi=_epi_swiglu,
                     out_dtype=BF16, tm=1024, tn=512, name=tag + "_up")
        return _mm_res(h, wo_b, l, x, scale=0.5, tm=512, tn=1024, name=tag + "_down")

    for l in range(DEPTH):
        x = ffn(x, ffn1_norm[l], ffn1_wi, ffn1_wo_b, l, f"l{l}_ffn1")

        nw = mix_norm[l]
        w_mid = w_in[l, :, o[3]:o[9]].astype(BF16)
        proj = _norm_mm(x, nw, [(w_mid, None, 0)], n=PROJ_MID_N, epi=_epi_id, out_dtype=F32,
                        tm=1024, tn=1024, name=f"l{l}_in_main")
        w_small = jnp.concatenate([w_in[l, :, o[0]:o[2]], _dt_place(w_in[l, :, o[9]:o[10]])],
                                  axis=1).astype(BF16)
        lat_dt = _norm_mm(x, nw, [(w_small, None, 0)], n=SMALL_N, epi=_epi_id, out_dtype=F32,
                          tm=1024, tn=512, name=f"l{l}_in_small")
        w_kpe = jnp.pad(w_in[l, :, o[2]:o[3]], ((0, 0), (0, LANES - MLA_ROPE))).astype(BF16)
        k_rot = _norm_mm(x, nw, [(w_kpe, None, 0)], n=LANES, epi=_epi_rope_k, out_dtype=BF16,
                         tm=1024, tn=LANES, extras=rope, name=f"l{l}_in_kpe")
        w_gates = w_in[l, :, o[10]:o[11]].astype(BF16)
        gates = _norm_mm(x, nw, [(w_gates, None, 0)], n=3 * D_MODEL, epi=_epi_sigmoid,
                         out_dtype=BF16, tm=1024, tn=1024, name=f"l{l}_in_gates")

        wq = mla_w_uq[l].reshape(MLA_RANK, MLA_HEADS, MLA_NOPE + MLA_ROPE)
        wq = jnp.pad(wq, ((0, 0), (0, 0), (0, MLA_QK_PAD - MLA_NOPE - MLA_ROPE)))
        wq = wq.reshape(MLA_RANK, MLA_HEADS * MLA_QK_PAD).astype(BF16)
        q = _norm_mm(lat_dt, mla_q_norm[l], [(wq, None, 0)], n=MLA_HEADS * MLA_QK_PAD,
                     epi=functools.partial(_epi_rope_q, scale=qk_scale), out_dtype=BF16,
                     tm=1024, tn=2048, xcol=0, extras=rope, name=f"l{l}_mla_q")
        wkv = mla_w_ukv[l].reshape(MLA_RANK, MLA_HEADS, MLA_NOPE + MLA_V)
        wkv = jnp.concatenate([wkv[:, :, :MLA_NOPE].reshape(MLA_RANK, -1),
                               wkv[:, :, MLA_NOPE:].reshape(MLA_RANK, -1)], axis=1).astype(BF16)
        kv = _norm_mm(lat_dt, mla_kv_norm[l], [(wkv, None, 0)], n=wkv.shape[1], epi=_epi_id,
                      out_dtype=BF16, tm=1024, tn=2048, xcol=1, name=f"l{l}_mla_kv")
        y_a = _flash(q, kv, k_rot, t=FLASH_T, nh=FLASH_HEADS, name=f"l{l}_mla_attn")

        y_b = _hgrn(proj, hgrn_lb_logits, hgrn_norm[l], col0=PROJ_HG, layer=l, n_chunks=4,
                    n_heads=4, name=f"l{l}_hgrn")

        conv = _conv(proj, ssm_conv_w[l], ssm_conv_b[l], col0=PROJ_XBC, tb=256, tc=1024,
                     name=f"l{l}_ssm_conv")
        y_c = _ssd(conv, proj, PROJ_Z, lat_dt, SMALL_DT,
                   _dt_place(ssm_dt_bias[l:l + 1]), _dt_place(ssm_a_log[l:l + 1]),
                   jnp.repeat(ssm_d[l], SSM_P).reshape(1, -1), ssm_norm[l],
                   n_chunks=4, name=f"l{l}_ssm_scan")

        m = _merge(y_a, y_b, y_c, *w_o_b, l, gates, tm=512, tn=512, name=f"l{l}_merge")
        x = _mm_res(m, w_out, l, x, scale=1.0, tm=1024, tn=1024, name=f"l{l}_out")

        x = ffn(x, ffn2_norm[l], ffn2_wi, ffn2_wo_b, l, f"l{l}_ffn2")

    return _rmsnorm(x, final_norm, tm=512, name="final_norm")[None]
```
